```python
import jax, jax.numpy as jnp
from jax import lax
import numpy as np

D_MODEL = 1024
BATCH = 8
SEQ = 4096
DEPTH = 4

N_RET_HEADS = 4
RET_HEAD_DIM = 128
RET_WIDTH = N_RET_HEADS * RET_HEAD_DIM
POOL_WINDOWS = (2, 4, 8, 16)
POOL_GROUP_DIM = 128
POOL_WIDTH = len(POOL_WINDOWS) * POOL_GROUP_DIM
MIX_WIDTH = RET_WIDTH + POOL_WIDTH
IN_WIDTH = 4 * RET_WIDTH + POOL_WIDTH
CHUNK = 128
ROPE_BASE = 10000.0
CONV_WIDTH = 31
D_FF = -(-8 * D_MODEL // (3 * 256)) * 256
EPS = 1e-6
N_EVEN = (DEPTH + 1) // 2
N_ODD = DEPTH // 2

kernel_name = "retention_pool_conformer_hybrid"


def rms_norm(x, g):
    x32 = x.astype(jnp.float32)
    y = x32 * lax.rsqrt(jnp.mean(x32 * x32, axis=-1, keepdims=True) + EPS)
    return (y * g).astype(x.dtype)


def rope(t, positions):
    half = t.shape[-1] // 2
    inv_freq = ROPE_BASE ** (-jnp.arange(half, dtype=jnp.float32) / half)
    ang = positions.astype(jnp.float32)[..., None] * inv_freq
    cos = jnp.cos(ang)[:, :, None, :]
    sin = jnp.sin(ang)[:, :, None, :]
    t1, t2 = t[..., :half], t[..., half:]
    return jnp.concatenate([t1 * cos - t2 * sin, t1 * sin + t2 * cos], axis=-1)


def retention(q, k, v, positions):
    B, S, H, Dh = q.shape
    n_chunks = S // CHUNK
    q = rope(q, positions)
    k = rope(k, positions) * (Dh ** -0.5)
    log_gamma = jnp.log1p(-(2.0 ** (-5.0 - jnp.arange(H, dtype=jnp.float32))))
    idx = jnp.arange(CHUNK, dtype=jnp.float32)
    rel = idx[:, None] - idx[None, :]
    decay_mask = jnp.where(rel >= 0, jnp.exp(log_gamma[:, None, None] * jnp.maximum(rel, 0.0)), 0.0)
    q_decay = jnp.exp(log_gamma[:, None] * (idx + 1.0))
    k_decay = jnp.exp(log_gamma[:, None] * (CHUNK - 1.0 - idx))
    chunk_decay = jnp.exp(log_gamma * CHUNK)

    def to_chunks(t):
        return t.reshape(B, n_chunks, CHUNK, H, Dh).transpose(0, 3, 1, 2, 4)

    qc, kc, vc = to_chunks(q), to_chunks(k), to_chunks(v)
    scores = jnp.einsum('bhncd,bhnmd->bhncm', qc, kc) * decay_mask[:, None]
    o_intra = jnp.einsum('bhncm,bhnme->bhnce', scores, vc)
    kv = jnp.einsum('bhnmd,bhnme->nbhde', kc * k_decay[:, None, :, None], vc)

    def step(state, kv_n):
        return state * chunk_decay[None, :, None, None] + kv_n, state

    _, prev_states = lax.scan(step, jnp.zeros((B, H, Dh, Dh), jnp.float32), kv)
    o_cross = jnp.einsum('bhncd,nbhde->bhnce', qc * q_decay[:, None, :, None], prev_states)
    return (o_intra + o_cross).transpose(0, 2, 3, 1, 4).reshape(B, S, H, Dh)


def multiscale_pool(u, pool_w, pool_scale):
    B, S, _ = u.shape
    u32 = u.astype(jnp.float32)
    cs0 = jnp.pad(jnp.cumsum(u32, axis=1), ((0, 0), (1, 0), (0, 0)))
    t = jnp.arange(S)
    outs = []
    for gi, w in enumerate(POOL_WINDOWS):
        c = cs0[:, :, gi * POOL_GROUP_DIM:(gi + 1) * POOL_GROUP_DIM]
        lower = jnp.pad(c[:, :S + 1 - w], ((0, 0), (w - 1, 0), (0, 0)))
        count = jnp.minimum(t + 1, w).astype(jnp.float32)[None, :, None]
        y = (c[:, 1:] - lower) / count - u32[:, :, gi * POOL_GROUP_DIM:(gi + 1) * POOL_GROUP_DIM]
        outs.append(jnp.einsum('bsc,cd->bsd', y.astype(u.dtype), pool_w[gi]))
    return jnp.concatenate(outs, axis=-1) * pool_scale


def retention_pool_mixer(h, positions, w_in, ret_norm_g, pool_w, pool_scale, w_out):
    B, S, _ = h.shape
    proj = h @ w_in
    q, k, v, g, u = jnp.split(proj, [RET_WIDTH, 2 * RET_WIDTH, 3 * RET_WIDTH, 4 * RET_WIDTH], axis=-1)

    def heads(t):
        return t.reshape(B, S, N_RET_HEADS, RET_HEAD_DIM).astype(jnp.float32)

    o = retention(heads(q), heads(k), heads(v), positions)
    mu = jnp.mean(o, axis=-1, keepdims=True)
    var = jnp.mean(jnp.square(o - mu), axis=-1, keepdims=True)
    o = ((o - mu) * lax.rsqrt(var + EPS)).reshape(B, S, RET_WIDTH) * ret_norm_g
    ret_out = (jax.nn.silu(g.astype(jnp.float32)) * o).astype(h.dtype)
    pool_out = multiscale_pool(u, pool_w, pool_scale).astype(h.dtype)
    return jnp.concatenate([ret_out, pool_out], axis=-1) @ w_out


def conformer_conv(h, w_pw1, b_pw1, w_dw, b_dw, ln_g, ln_b, w_pw2, b_pw2):
    a, gate = jnp.split(h @ w_pw1 + b_pw1, 2, axis=-1)
    u = a * jax.nn.sigmoid(gate)
    dw = lax.conv_general_dilated(
        u, w_dw[:, None, :], window_strides=(1,), padding=((CONV_WIDTH - 1, 0),),
        dimension_numbers=('NWC', 'WIO', 'NWC'), feature_group_count=D_MODEL) + b_dw
    d32 = dw.astype(jnp.float32)
    mu = jnp.mean(d32, axis=-1, keepdims=True)
    var = jnp.mean(jnp.square(d32 - mu), axis=-1, keepdims=True)
    z = (d32 - mu) * lax.rsqrt(var + EPS) * ln_g + ln_b
    z = jax.nn.silu(z).astype(h.dtype)
    return z @ w_pw2 + b_pw2


def swiglu(h, w_gate, w_up, w_down):
    return (jax.nn.silu(h @ w_gate) * (h @ w_up)) @ w_down


def setup_inputs(seed: int = 0) -> dict:
    key = jax.random.key(seed)
    ks = jax.random.split(key, 24)
    f32 = jnp.float32

    def dense(k, shape, fan_in):
        return jax.random.normal(k, shape, f32) * (fan_in ** -0.5)

    def gain(k, shape):
        return 1.0 + 0.05 * jax.random.normal(k, shape, f32)

    def bias(k, shape):
        return 0.01 * jax.random.normal(k, shape, f32)

    return {
        "x": jax.random.normal(ks[0], (BATCH, SEQ, D_MODEL), f32),
        "positions": jnp.broadcast_to(jnp.arange(SEQ, dtype=jnp.int32), (BATCH, SEQ)),
        "mixer_norm_g": gain(ks[1], (DEPTH, D_MODEL)),
        "ffn_norm_g": gain(ks[2], (DEPTH, D_MODEL)),
        "final_norm_g": gain(ks[3], (D_MODEL,)),
        "ret_w_in": dense(ks[4], (N_EVEN, D_MODEL, IN_WIDTH), D_MODEL),
        "ret_norm_g": gain(ks[5], (N_EVEN, RET_WIDTH)),
        "pool_w": dense(ks[6], (N_EVEN, len(POOL_WINDOWS), POOL_GROUP_DIM, POOL_GROUP_DIM), POOL_GROUP_DIM),
        "pool_scale": gain(ks[7], (N_EVEN, POOL_WIDTH)),
        "mix_w_out": dense(ks[8], (N_EVEN, MIX_WIDTH, D_MODEL), MIX_WIDTH),
        "conv_w_pw1": dense(ks[9], (N_ODD, D_MODEL, 2 * D_MODEL), D_MODEL),
        "conv_b_pw1": bias(ks[10], (N_ODD, 2 * D_MODEL)),
        "conv_w_dw": dense(ks[11], (N_ODD, CONV_WIDTH, D_MODEL), CONV_WIDTH),
        "conv_b_dw": bias(ks[12], (N_ODD, D_MODEL)),
        "conv_ln_g": gain(ks[13], (N_ODD, D_MODEL)),
        "conv_ln_b": bias(ks[14], (N_ODD, D_MODEL)),
        "conv_w_pw2": dense(ks[15], (N_ODD, D_MODEL, D_MODEL), D_MODEL),
        "conv_b_pw2": bias(ks[16], (N_ODD, D_MODEL)),
        "ffn_w_gate": dense(ks[17], (DEPTH, D_MODEL, D_FF), D_MODEL),
        "ffn_w_up": dense(ks[18], (DEPTH, D_MODEL, D_FF), D_MODEL),
        "ffn_w_down": dense(ks[19], (DEPTH, D_FF, D_MODEL), D_FF),
    }


def reference(x, positions, mixer_norm_g, ffn_norm_g, final_norm_g, ret_w_in, ret_norm_g,
              pool_w, pool_scale, mix_w_out, conv_w_pw1, conv_b_pw1, conv_w_dw, conv_b_dw,
              conv_ln_g, conv_ln_b, conv_w_pw2, conv_b_pw2, ffn_w_gate, ffn_w_up, ffn_w_down):
    h = x
    for layer in range(DEPTH):
        i = layer // 2
        hn = rms_norm(h, mixer_norm_g[layer])
        if layer % 2 == 0:
            h = h + retention_pool_mixer(hn, positions, ret_w_in[i], ret_norm_g[i],
                                         pool_w[i], pool_scale[i], mix_w_out[i])
        else:
            h = h + conformer_conv(hn, conv_w_pw1[i], conv_b_pw1[i], conv_w_dw[i], conv_b_dw[i],
                                   conv_ln_g[i], conv_ln_b[i], conv_w_pw2[i], conv_b_pw2[i])
        h = h + swiglu(rms_norm(h, ffn_norm_g[layer]), ffn_w_gate[layer], ffn_w_up[layer], ffn_w_down[layer])
    return rms_norm(h, final_norm_g)
```

```python
import functools

import numpy as np
import jax
import jax.numpy as jnp
from jax import lax
from jax.experimental import pallas as pl
from jax.experimental.pallas import tpu as pltpu

D_MODEL = 1024
N_HEADS = 4
HEAD_DIM = 128
RET_WIDTH = N_HEADS * HEAD_DIM
POOL_WINDOWS = (2, 4, 8, 16)
POOL_GROUP = 128
POOL_WIDTH = len(POOL_WINDOWS) * POOL_GROUP
CHUNK = 128
ROPE_BASE = 10000.0
CONV_WIDTH = 31
EPS = 1e-6

LANES = 128
SUBLANES = 8
VMEM_LIMIT_BYTES = 56 * 1024 * 1024

TS = 512
FFN_COLS = 256

SEG = TS // SUBLANES
HALO = 32
Z_PITCH = 104
Y_PITCH = 72
N_SLABS = D_MODEL // LANES
CONV_PB = 4

POOL_HALO = 16

bf16 = jnp.bfloat16
f32 = jnp.float32


def _dot(a, b):
    return jnp.dot(a, b, preferred_element_type=f32)


def _sigmoid(x):
    return 1.0 / (1.0 + jnp.exp(-x))


def _rms_norm(x, g):
    ms = jnp.mean(x * x, axis=-1, keepdims=True)
    return x * lax.rsqrt(ms + EPS) * g


def _ffn_kernel(x_ref, g_ref, wg_ref, wu_ref, wd_ref, fg_ref, o_ref, a_ref, *, final):
    x = x_ref[0]
    hn = _rms_norm(x, g_ref[...]).astype(bf16)
    d_ff = wg_ref.shape[1]
    for c in range(d_ff // FFN_COLS):
        cols = pl.ds(c * FFN_COLS, FFN_COLS)
        gate = _dot(hn, wg_ref[:, cols])
        up = _dot(hn, wu_ref[:, cols])
        a_ref[:, cols] = (gate * _sigmoid(gate) * up).astype(bf16)
    y = x + _dot(a_ref[...], wd_ref[...])
    if final:
        y = _rms_norm(y, fg_ref[...])
    o_ref[0] = y


def _resident(shape):
    zeros = (0,) * len(shape)
    return pl.BlockSpec(shape, lambda b, j: zeros, pipeline_mode=pl.Buffered(1))


def _tile_spec(d):
    return pl.BlockSpec((1, TS, d), lambda b, j: (b, j, 0))


def _params():
    return pltpu.CompilerParams(
        dimension_semantics=("arbitrary", "arbitrary"),
        vmem_limit_bytes=VMEM_LIMIT_BYTES)


def _ffn(h, norm_g, w_gate, w_up, w_down, final_g, final):
    B, S, D = h.shape
    d_ff = w_gate.shape[1]
    return pl.pallas_call(
        functools.partial(_ffn_kernel, final=final),
        grid=(B, S // TS),
        in_specs=[_tile_spec(D), _resident((1, D)), _resident((D, d_ff)), _resident((D, d_ff)),
                  _resident((d_ff, D)), _resident((1, D))],
        out_specs=_tile_spec(D),
        out_shape=jax.ShapeDtypeStruct(h.shape, h.dtype),
        scratch_shapes=[pltpu.VMEM((TS, d_ff), bf16)],
        compiler_params=_params(),
        name="ffn_final" if final else "ffn",
    )(h, norm_g, w_gate, w_up, w_down, final_g)


def _even_kernel(x_ref, pos_ref, ng_ref, win_ref, invf_ref, sign_ref, mask_ref, qdec_ref, kdec_ref,
                 rng_ref, pw_ref, ps_ref, wout_ref, o_ref,
                 proj_ref, state_ref, uext_ref, mix_ref, *, chunk_decay):
    j = pl.program_id(1)

    @pl.when(j == 0)
    def _():
        state_ref[...] = jnp.zeros_like(state_ref)
        uext_ref[pl.ds(0, POOL_HALO), :] = jnp.zeros((POOL_HALO, POOL_WIDTH), f32)

    x = x_ref[0]
    hn = _rms_norm(x, ng_ref[...]).astype(bf16)
    proj_ref[...] = _dot(hn, win_ref[...])

    ang = pos_ref[0].astype(f32) * invf_ref[...]
    cos_t = jnp.cos(ang)
    sin_t = jnp.sin(ang) * sign_ref[...]

    def rope(t, cs, sn):
        return t * cs + pltpu.roll(t, HEAD_DIM // 2, axis=1) * sn

    for c in range(TS // CHUNK):
        rows = pl.ds(c * CHUNK, CHUNK)
        cs = cos_t[c * CHUNK:(c + 1) * CHUNK]
        sn = sin_t[c * CHUNK:(c + 1) * CHUNK]
        for h in range(N_HEADS):
            col = h * HEAD_DIM
            q = rope(proj_ref[rows, pl.ds(col, HEAD_DIM)], cs, sn)
            k = rope(proj_ref[rows, pl.ds(RET_WIDTH + col, HEAD_DIM)], cs, sn)
            v = proj_ref[rows, pl.ds(2 * RET_WIDTH + col, HEAD_DIM)].astype(bf16)
            gate = proj_ref[rows, pl.ds(3 * RET_WIDTH + col, HEAD_DIM)]
            kb = k.astype(bf16)
            scores = lax.dot_general(q.astype(bf16), kb, (((1,), (1,)), ((), ())),
                                     preferred_element_type=f32)
            scores = (scores * mask_ref[h]).astype(bf16)
            qd = (q * qdec_ref[h]).astype(bf16)
            st = state_ref[h]
            o = _dot(jnp.concatenate([scores, qd], axis=1),
                     jnp.concatenate([v, st.astype(bf16)], axis=0))
            kd = (k * kdec_ref[h]).astype(bf16)
            kv = lax.dot_general(kd, v, (((0,), (0,)), ((), ())), preferred_element_type=f32)
            state_ref[h] = st * chunk_decay[h] + kv
            mu = jnp.mean(o, axis=-1, keepdims=True)
            oc = o - mu
            var = jnp.mean(oc * oc, axis=-1, keepdims=True)
            on = oc * lax.rsqrt(var + EPS) * rng_ref[:, pl.ds(col, HEAD_DIM)]
            mix_ref[rows, pl.ds(col, HEAD_DIM)] = (gate * _sigmoid(gate) * on).astype(bf16)

    ucols = pl.ds(4 * RET_WIDTH, POOL_WIDTH)
    uext_ref[pl.ds(POOL_HALO, TS), :] = proj_ref[:, ucols]
    row = lax.broadcasted_iota(jnp.int32, (TS, POOL_GROUP), 0) + j * TS + 1
    for gi, w in enumerate(POOL_WINDOWS):
        gcols = pl.ds(gi * POOL_GROUP, POOL_GROUP)
        e = uext_ref[:, gcols]
        s = e
        step = 1
        while step < w:
            s = s + pltpu.roll(s, step, axis=0)
            step *= 2
        cnt = jnp.minimum(row, w).astype(f32)
        y = s[POOL_HALO:] / cnt - e[POOL_HALO:]
        pooled = _dot(y.astype(bf16), pw_ref[gi]) * ps_ref[:, gcols]
        mix_ref[:, pl.ds(RET_WIDTH + gi * POOL_GROUP, POOL_GROUP)] = pooled.astype(bf16)
    uext_ref[pl.ds(0, POOL_HALO), :] = uext_ref[pl.ds(TS, POOL_HALO), :]

    o_ref[0] = x + _dot(mix_ref[...], wout_ref[...])


def _retention_constants():
    idx = jnp.arange(CHUNK, dtype=f32)
    log_gamma = jnp.log1p(-(2.0 ** (-5.0 - jnp.arange(N_HEADS, dtype=f32))))
    rel = idx[:, None] - idx[None, :]
    mask = jnp.where(rel >= 0, jnp.exp(log_gamma[:, None, None] * jnp.maximum(rel, 0.0)), 0.0)
    scale = HEAD_DIM ** -0.5
    q_decay = jnp.exp(log_gamma[:, None] * (idx + 1.0))
    k_decay = jnp.exp(log_gamma[:, None] * (CHUNK - 1.0 - idx))
    ones = jnp.ones((1, 1, HEAD_DIM), f32)
    return (mask * scale, q_decay[:, :, None] * ones, (k_decay * scale)[:, :, None] * ones)


def _chunk_decay():
    lg = np.log1p(-(2.0 ** (-5.0 - np.arange(N_HEADS, dtype=np.float32)))).astype(np.float32)
    return tuple(float(v) for v in np.exp(lg * np.float32(CHUNK)).astype(np.float32))


def _even_mixer(h, positions, norm_g, w_in, ret_norm_g, pool_w, pool_scale, w_out):
    B, S, D = h.shape
    in_width = w_in.shape[1]
    half = HEAD_DIM // 2
    inv_freq = ROPE_BASE ** (-jnp.arange(half, dtype=f32) / half)
    inv_freq = jnp.concatenate([inv_freq, inv_freq])[None, :]
    sign = jnp.concatenate([-jnp.ones((half,), f32), jnp.ones((half,), f32)])[None, :]
    mask, q_decay, k_decay = _retention_constants()
    cst = (N_HEADS, CHUNK, HEAD_DIM)
    return pl.pallas_call(
        functools.partial(_even_kernel, chunk_decay=_chunk_decay()),
        grid=(B, S // TS),
        in_specs=[_tile_spec(D),
                  pl.BlockSpec((1, TS, 1), lambda b, j: (b, j, 0)),
                  _resident((1, D)), _resident((D, in_width)),
                  _resident((1, HEAD_DIM)), _resident((1, HEAD_DIM)),
                  _resident(cst), _resident(cst), _resident(cst),
                  _resident((1, RET_WIDTH)),
                  _resident((len(POOL_WINDOWS), POOL_GROUP, POOL_GROUP)),
                  _resident((1, POOL_WIDTH)),
                  _resident((RET_WIDTH + POOL_WIDTH, D))],
        out_specs=_tile_spec(D),
        out_shape=jax.ShapeDtypeStruct(h.shape, h.dtype),
        scratch_shapes=[pltpu.VMEM((TS, in_width), f32),
                        pltpu.VMEM((N_HEADS, HEAD_DIM, HEAD_DIM), f32),
                        pltpu.VMEM((POOL_HALO + TS, POOL_WIDTH), f32),
                        pltpu.VMEM((TS, RET_WIDTH + POOL_WIDTH), bf16)],
        compiler_params=_params(),
        name="even_mixer",
    )(h, positions, norm_g, w_in, inv_freq, sign, mask, q_decay, k_decay,
      ret_norm_g, pool_w, pool_scale, w_out)


def _odd_kernel(x_ref, ng_ref, w1_ref, b1_ref, wdw_ref, bdw_ref, lng_ref, lnb_ref, w2_ref, b2_ref,
                o_ref, z_ref, e_ref, y_ref, carry_ref):
    j = pl.program_id(1)

    @pl.when(j == 0)
    def _():
        carry_ref[...] = jnp.zeros_like(carry_ref)

    x = x_ref[0]
    hn = _rms_norm(x, ng_ref[...]).astype(bf16)
    a = _dot(hn, w1_ref[:, pl.ds(0, D_MODEL)]) + b1_ref[:, pl.ds(0, D_MODEL)]
    gt = _dot(hn, w1_ref[:, pl.ds(D_MODEL, D_MODEL)]) + b1_ref[:, pl.ds(D_MODEL, D_MODEL)]
    u = a * _sigmoid(gt)

    for c in range(N_SLABS):
        uc = u[:, c * LANES:(c + 1) * LANES]
        z_ref[c, pl.ds(0, HALO), :] = carry_ref[c]
        for s in range(SUBLANES):
            z_ref[c, pl.ds(s * Z_PITCH + HALO, SEG), :] = uc[s * SEG:(s + 1) * SEG]
            tail = uc[(s + 1) * SEG - HALO:(s + 1) * SEG]
            if s + 1 < SUBLANES:
                z_ref[c, pl.ds((s + 1) * Z_PITCH, HALO), :] = tail
            else:
                carry_ref[c] = tail

    def build(q, carry):
        for c in range(N_SLABS):
            e_ref[c, pl.ds(pl.multiple_of(q * SUBLANES, SUBLANES), SUBLANES), :] = (
                z_ref[c, pl.ds(q, SUBLANES, stride=Z_PITCH), :])
        return carry

    lax.fori_loop(0, SEG + HALO, build, 0)

    base = HALO - (CONV_WIDTH - 1)
    for c in range(N_SLABS):
        lanes = pl.ds(c * LANES, LANES)
        wv = [jnp.broadcast_to(wdw_ref[pl.ds(k, 1), lanes], (SUBLANES, LANES)) for k in range(CONV_WIDTH)]
        bias = jnp.broadcast_to(bdw_ref[:, lanes], (SUBLANES, LANES))

        def conv(pb, carry, c=c, wv=wv, bias=bias):
            p0 = pb * CONV_PB
            ev = [e_ref[c, pl.ds(pl.multiple_of((p0 + base + m) * SUBLANES, SUBLANES), SUBLANES), :]
                  for m in range(CONV_PB + CONV_WIDTH - 1)]
            for i in range(CONV_PB):
                acc = bias
                for k in range(CONV_WIDTH):
                    acc = acc + ev[i + k] * wv[k]
                y_ref[c, pl.ds(p0 + i, SUBLANES, stride=Y_PITCH), :] = acc
            return carry

        lax.fori_loop(0, SEG // CONV_PB, conv, 0)

    d = jnp.concatenate(
        [jnp.concatenate([y_ref[c, pl.ds(s * Y_PITCH, SEG), :] for c in range(N_SLABS)], axis=1)
         for s in range(SUBLANES)], axis=0)
    mu = jnp.mean(d, axis=-1, keepdims=True)
    dc = d - mu
    var = jnp.mean(dc * dc, axis=-1, keepdims=True)
    zz = dc * lax.rsqrt(var + EPS) * lng_ref[...] + lnb_ref[...]
    zz = (zz * _sigmoid(zz)).astype(bf16)
    o_ref[0] = x + _dot(zz, w2_ref[...]) + b2_ref[...]


def _odd_mixer(h, norm_g, w1, b1, w_dw, b_dw, ln_g, ln_b, w2, b2):
    B, S, D = h.shape
    return pl.pallas_call(
        _odd_kernel,
        grid=(B, S // TS),
        in_specs=[_tile_spec(D), _resident((1, D)), _resident((D, 2 * D)), _resident((1, 2 * D)),
                  _resident((CONV_WIDTH, D)), _resident((1, D)), _resident((1, D)), _resident((1, D)),
                  _resident((D, D)), _resident((1, D))],
        out_specs=_tile_spec(D),
        out_shape=jax.ShapeDtypeStruct(h.shape, h.dtype),
        scratch_shapes=[pltpu.VMEM((N_SLABS, SUBLANES * Z_PITCH, LANES), f32),
                        pltpu.VMEM((N_SLABS, (SEG + HALO) * SUBLANES, LANES), f32),
                        pltpu.VMEM((N_SLABS, SUBLANES * Y_PITCH, LANES), f32),
                        pltpu.VMEM((N_SLABS, HALO, LANES), f32)],
        compiler_params=_params(),
        name="odd_mixer",
    )(h, norm_g, w1, b1, w_dw, b_dw, ln_g, ln_b, w2, b2)


def kernel(x, positions, mixer_norm_g, ffn_norm_g, final_norm_g, ret_w_in, ret_norm_g, pool_w, pool_scale, mix_w_out, conv_w_pw1, conv_b_pw1, conv_w_dw, conv_b_dw, conv_ln_g, conv_ln_b, conv_w_pw2, conv_b_pw2, ffn_w_gate, ffn_w_up, ffn_w_down):
    depth = mixer_norm_g.shape[0]
    pos = positions[:, :, None]
    row = lambda v: v[None, :]
    h = x
    for layer in range(depth):
        i = layer // 2
        if layer % 2 == 0:
            h = _even_mixer(h, pos, row(mixer_norm_g[layer]), ret_w_in[i].astype(bf16),
                            row(ret_norm_g[i]), pool_w[i].astype(bf16), row(pool_scale[i]),
                            mix_w_out[i].astype(bf16))
        else:
            h = _odd_mixer(h, row(mixer_norm_g[layer]), conv_w_pw1[i].astype(bf16), row(conv_b_pw1[i]),
                           conv_w_dw[i], row(conv_b_dw[i]), row(conv_ln_g[i]), row(conv_ln_b[i]),
                           conv_w_pw2[i].astype(bf16), row(conv_b_pw2[i]))
        h = _ffn(h, row(ffn_norm_g[layer]), ffn_w_gate[layer].astype(bf16), ffn_w_up[layer].astype(bf16),
                 ffn_w_down[layer].astype(bf16), row(final_norm_g), final=(layer == depth - 1))
    return h
```

```python
import functools

import numpy as np
import jax
import jax.numpy as jnp
from jax import lax
from jax.experimental import pallas as pl
from jax.experimental.pallas import tpu as pltpu

D_MODEL = 1024
N_HEADS = 4
HEAD_DIM = 128
RET_WIDTH = N_HEADS * HEAD_DIM
POOL_WINDOWS = (2, 4, 8, 16)
POOL_GROUP = 128
POOL_WIDTH = len(POOL_WINDOWS) * POOL_GROUP
CHUNK = 128
ROPE_BASE = 10000.0
CONV_WIDTH = 31
EPS = 1e-6

LANES = 128
SUBLANES = 8
MXU_COLS = 256
VMEM_LIMIT_BYTES = 56 * 1024 * 1024

TS = 256

SEG = TS // SUBLANES
HALO = 32
Z_PITCH = 72
Y_PITCH = 40
N_SLABS = D_MODEL // LANES
CONV_PB = 4

POOL_HALO = 16

bf16 = jnp.bfloat16
f32 = jnp.float32


def _dot(a, b):
    return jnp.dot(a, b, preferred_element_type=f32)


def _sigmoid(x):
    return 1.0 / (1.0 + jnp.exp(-x))


def _rms_norm(x, g):
    ms = jnp.mean(x * x, axis=-1, keepdims=True)
    return x * lax.rsqrt(ms + EPS) * g


def _mxu_cost(m, k, n):
    return m * k * max(n, MXU_COLS)


def _ffn_stream(h1_ref, g_ref, wg_ref, wu_ref, wd_ref, fg_ref, hn_ref, act_ref, o_ref, final):
    x = h1_ref[...]
    hn_ref[...] = _rms_norm(x, g_ref[...]).astype(bf16)
    o_ref[0] = x
    yield 0
    d_ff = wg_ref.shape[1]
    for c in range(d_ff // MXU_COLS):
        cols = pl.ds(c * MXU_COLS, MXU_COLS)
        gate = _dot(hn_ref[...], wg_ref[:, cols])
        up = _dot(hn_ref[...], wu_ref[:, cols])
        act_ref[:, cols] = (gate * _sigmoid(gate) * up).astype(bf16)
        yield 2 * _mxu_cost(TS, D_MODEL, MXU_COLS)
    for c in range(D_MODEL // MXU_COLS):
        cols = pl.ds(c * MXU_COLS, MXU_COLS)
        o_ref[0, :, cols] = o_ref[0, :, cols] + _dot(act_ref[...], wd_ref[:, cols])
        yield _mxu_cost(TS, d_ff, MXU_COLS)
    if final:
        o_ref[0] = _rms_norm(o_ref[0], fg_ref[...])


def _ffn_total(d_ff):
    return 3 * _mxu_cost(TS, D_MODEL, d_ff)


def _even_init(first, state_ref, uext_ref):
    @pl.when(first)
    def _():
        state_ref[...] = jnp.zeros_like(state_ref)
        uext_ref[pl.ds(0, POOL_HALO), :] = jnp.zeros((POOL_HALO, POOL_WIDTH), f32)


def _even_stream(x_ref, h1_ref, tile_in_seq, pos_ref, ng_ref, win_ref, invf_ref, sign_ref, mask_ref,
                 qdec_ref, kdec_ref, rng_ref, pw_ref, ps_ref, wout_ref,
                 hn_ref, proj_ref, state_ref, uext_ref, mix_ref, chunk_decay):
    hn_ref[...] = _rms_norm(x_ref[0], ng_ref[...]).astype(bf16)
    in_width = win_ref.shape[1]
    for part in range(in_width // RET_WIDTH):
        cols = pl.ds(part * RET_WIDTH, RET_WIDTH)
        proj_ref[:, cols] = _dot(hn_ref[...], win_ref[:, cols])
        yield _mxu_cost(TS, D_MODEL, RET_WIDTH)

    ang = pos_ref[0].astype(f32) * invf_ref[...]
    cos_t = jnp.cos(ang)
    sin_t = jnp.sin(ang) * sign_ref[...]

    def rope(t, cs, sn):
        return t * cs + pltpu.roll(t, HEAD_DIM // 2, axis=1) * sn

    heads = range(N_HEADS)
    for c in range(TS // CHUNK):
        rows = pl.ds(c * CHUNK, CHUNK)
        cs = cos_t[c * CHUNK:(c + 1) * CHUNK]
        sn = sin_t[c * CHUNK:(c + 1) * CHUNK]
        q = [rope(proj_ref[rows, pl.ds(h * HEAD_DIM, HEAD_DIM)], cs, sn) for h in heads]
        k = [rope(proj_ref[rows, pl.ds(RET_WIDTH + h * HEAD_DIM, HEAD_DIM)], cs, sn) for h in heads]
        v = [proj_ref[rows, pl.ds(2 * RET_WIDTH + h * HEAD_DIM, HEAD_DIM)].astype(bf16) for h in heads]
        scores = [lax.dot_general(q[h].astype(bf16), k[h].astype(bf16), (((1,), (1,)), ((), ())),
                                  preferred_element_type=f32) for h in heads]
        yield N_HEADS * _mxu_cost(CHUNK, HEAD_DIM, CHUNK)
        o = []
        for h in heads:
            sm = (scores[h] * mask_ref[h]).astype(bf16)
            qd = (q[h] * qdec_ref[h]).astype(bf16)
            st = state_ref[h]
            o.append(_dot(jnp.concatenate([sm, qd], axis=1),
                          jnp.concatenate([v[h], st.astype(bf16)], axis=0)))
            kd = (k[h] * kdec_ref[h]).astype(bf16)
            kv = lax.dot_general(kd, v[h], (((0,), (0,)), ((), ())), preferred_element_type=f32)
            state_ref[h] = st * chunk_decay[h] + kv
        yield N_HEADS * (_mxu_cost(CHUNK, 2 * CHUNK, HEAD_DIM) + _mxu_cost(HEAD_DIM, CHUNK, HEAD_DIM))
        for h in heads:
            col = pl.ds(h * HEAD_DIM, HEAD_DIM)
            gate = proj_ref[rows, pl.ds(3 * RET_WIDTH + h * HEAD_DIM, HEAD_DIM)]
            mu = jnp.mean(o[h], axis=-1, keepdims=True)
            oc = o[h] - mu
            var = jnp.mean(oc * oc, axis=-1, keepdims=True)
            on = oc * lax.rsqrt(var + EPS) * rng_ref[:, col]
            mix_ref[rows, col] = (gate * _sigmoid(gate) * on).astype(bf16)

    ucols = pl.ds(4 * RET_WIDTH, POOL_WIDTH)
    uext_ref[pl.ds(POOL_HALO, TS), :] = proj_ref[:, ucols]
    row = lax.broadcasted_iota(jnp.int32, (TS, POOL_GROUP), 0) + tile_in_seq * TS + 1
    for gi, w in enumerate(POOL_WINDOWS):
        gcols = pl.ds(gi * POOL_GROUP, POOL_GROUP)
        e = uext_ref[:, gcols]
        s = e
        step = 1
        while step < w:
            s = s + pltpu.roll(s, step, axis=0)
            step *= 2
        cnt = jnp.minimum(row, w).astype(f32)
        y = s[POOL_HALO:] / cnt - e[POOL_HALO:]
        pooled = _dot(y.astype(bf16), pw_ref[gi]) * ps_ref[:, gcols]
        mix_ref[:, pl.ds(RET_WIDTH + gi * POOL_GROUP, POOL_GROUP)] = pooled.astype(bf16)
    uext_ref[pl.ds(0, POOL_HALO), :] = uext_ref[pl.ds(TS, POOL_HALO), :]
    yield len(POOL_WINDOWS) * _mxu_cost(TS, POOL_GROUP, POOL_GROUP)

    for c in range(D_MODEL // MXU_COLS):
        cols = pl.ds(c * MXU_COLS, MXU_COLS)
        h1_ref[:, cols] = x_ref[0, :, cols] + _dot(mix_ref[...], wout_ref[:, cols])
        yield _mxu_cost(TS, RET_WIDTH + POOL_WIDTH, MXU_COLS)


def _even_total(in_width):
    per_chunk = N_HEADS * (_mxu_cost(CHUNK, HEAD_DIM, CHUNK) + _mxu_cost(CHUNK, 2 * CHUNK, HEAD_DIM)
                           + _mxu_cost(HEAD_DIM, CHUNK, HEAD_DIM))
    return (_mxu_cost(TS, D_MODEL, in_width) + (TS // CHUNK) * per_chunk
            + len(POOL_WINDOWS) * _mxu_cost(TS, POOL_GROUP, POOL_GROUP)
            + _mxu_cost(TS, RET_WIDTH + POOL_WIDTH, D_MODEL))


def _retention_constants():
    idx = jnp.arange(CHUNK, dtype=f32)
    log_gamma = jnp.log1p(-(2.0 ** (-5.0 - jnp.arange(N_HEADS, dtype=f32))))
    rel = idx[:, None] - idx[None, :]
    mask = jnp.where(rel >= 0, jnp.exp(log_gamma[:, None, None] * jnp.maximum(rel, 0.0)), 0.0)
    scale = HEAD_DIM ** -0.5
    q_decay = jnp.exp(log_gamma[:, None] * (idx + 1.0))
    k_decay = jnp.exp(log_gamma[:, None] * (CHUNK - 1.0 - idx))
    ones = jnp.ones((1, 1, HEAD_DIM), f32)
    return (mask * scale, q_decay[:, :, None] * ones, (k_decay * scale)[:, :, None] * ones)


def _chunk_decay():
    lg = np.log1p(-(2.0 ** (-5.0 - np.arange(N_HEADS, dtype=np.float32)))).astype(np.float32)
    return tuple(float(v) for v in np.exp(lg * np.float32(CHUNK)).astype(np.float32))


def _odd_init(first, carry_ref):
    @pl.when(first)
    def _():
        carry_ref[...] = jnp.zeros_like(carry_ref)


def _odd_stream(x_ref, h1_ref, ng_ref, w1_ref, b1_ref, wdw_ref, bdw_ref, lng_ref, lnb_ref, w2_ref, b2_ref,
                hn_ref, z_ref, e_ref, y_ref, carry_ref):
    hn_ref[...] = _rms_norm(x_ref[0], ng_ref[...]).astype(bf16)
    base = HALO - (CONV_WIDTH - 1)
    n_e = SEG + CONV_WIDTH - 1
    per_blk = MXU_COLS // LANES

    for c2 in range(D_MODEL // MXU_COLS):
        cols = pl.ds(c2 * MXU_COLS, MXU_COLS)
        gcols = pl.ds(D_MODEL + c2 * MXU_COLS, MXU_COLS)
        a = _dot(hn_ref[...], w1_ref[:, cols]) + b1_ref[:, cols]
        gt = _dot(hn_ref[...], w1_ref[:, gcols]) + b1_ref[:, gcols]
        yield 2 * _mxu_cost(TS, D_MODEL, MXU_COLS)
        u2 = a * _sigmoid(gt)
        for ci in range(per_blk):
            c = c2 * per_blk + ci
            uc = u2[:, ci * LANES:(ci + 1) * LANES]
            z_ref[c, pl.ds(0, HALO), :] = carry_ref[c]
            for s in range(SUBLANES):
                z_ref[c, pl.ds(s * Z_PITCH + HALO, SEG), :] = uc[s * SEG:(s + 1) * SEG]
                tail = uc[(s + 1) * SEG - HALO:(s + 1) * SEG]
                if s + 1 < SUBLANES:
                    z_ref[c, pl.ds((s + 1) * Z_PITCH, HALO), :] = tail
                else:
                    carry_ref[c] = tail
            for m in range(n_e):
                e_ref[c, pl.ds(m * SUBLANES, SUBLANES), :] = z_ref[c, pl.ds(base + m, SUBLANES, stride=Z_PITCH), :]
            lanes = pl.ds(c * LANES, LANES)
            wv = [jnp.broadcast_to(wdw_ref[pl.ds(k, 1), lanes], (SUBLANES, LANES)) for k in range(CONV_WIDTH)]
            bias = jnp.broadcast_to(bdw_ref[:, lanes], (SUBLANES, LANES))
            for p0 in range(0, SEG, CONV_PB):
                ev = [e_ref[c, pl.ds((p0 + m) * SUBLANES, SUBLANES), :] for m in range(CONV_PB + CONV_WIDTH - 1)]
                for i in range(CONV_PB):
                    acc = bias
                    for k in range(CONV_WIDTH):
                        acc = acc + ev[i + k] * wv[k]
                    y_ref[c, pl.ds(p0 + i, SUBLANES, stride=Y_PITCH), :] = acc

    d = jnp.concatenate(
        [jnp.concatenate([y_ref[c, pl.ds(s * Y_PITCH, SEG), :] for c in range(N_SLABS)], axis=1)
         for s in range(SUBLANES)], axis=0)
    mu = jnp.mean(d, axis=-1, keepdims=True)
    dc = d - mu
    var = jnp.mean(dc * dc, axis=-1, keepdims=True)
    zz = dc * lax.rsqrt(var + EPS) * lng_ref[...] + lnb_ref[...]
    hn_ref[...] = (zz * _sigmoid(zz)).astype(bf16)
    for c in range(D_MODEL // MXU_COLS):
        cols = pl.ds(c * MXU_COLS, MXU_COLS)
        h1_ref[:, cols] = x_ref[0, :, cols] + _dot(hn_ref[...], w2_ref[:, cols]) + b2_ref[:, cols]
        yield _mxu_cost(TS, D_MODEL, MXU_COLS)


def _odd_total():
    return 3 * _mxu_cost(TS, D_MODEL, D_MODEL)


N_FFN_IN = 5


def _interleave(streams):
    done = [0.0] * len(streams)
    alive = list(range(len(streams)))
    while alive:
        i = min(alive, key=lambda a: done[a] / streams[a][1])
        try:
            done[i] += next(streams[i][0])
        except StopIteration:
            alive.remove(i)


def _layer_kernel(*refs, kind, n_mix_in, n_mix_scratch, tiles_per_seq, final, chunk_decay):
    x_ref = refs[0]
    mix_in = refs[1:1 + n_mix_in]
    ffn_in = refs[1 + n_mix_in:1 + n_mix_in + N_FFN_IN]
    o_ref = refs[1 + n_mix_in + N_FFN_IN]
    scratch = refs[2 + n_mix_in + N_FFN_IN:]
    h1_ref, hnf_ref, act_ref = scratch[:3]
    mix_scratch = scratch[3:3 + n_mix_scratch]

    n = pl.program_id(0)
    tile_in_seq = lax.rem(n, tiles_per_seq)
    first = tile_in_seq == 0

    @pl.when(n == 0)
    def _():
        h1_ref[...] = jnp.zeros((TS, D_MODEL), f32)

    if kind == "even":
        _even_init(first, mix_scratch[2], mix_scratch[3])
        mixer = _even_stream(x_ref, h1_ref, tile_in_seq, *mix_in, *mix_scratch, chunk_decay)
        mixer_total = _even_total(mix_in[2].shape[1])
    else:
        _odd_init(first, mix_scratch[4])
        mixer = _odd_stream(x_ref, h1_ref, *mix_in, *mix_scratch)
        mixer_total = _odd_total()

    ffn = _ffn_stream(h1_ref, *ffn_in, hnf_ref, act_ref, o_ref, final)
    next(ffn)
    _interleave([[ffn, _ffn_total(ffn_in[1].shape[1])], [mixer, mixer_total]])


def _resident(shape):
    zeros = (0,) * len(shape)
    return pl.BlockSpec(shape, lambda n: zeros, pipeline_mode=pl.Buffered(1))


def _layer(kind, h, mix_args, mix_specs, mix_scratch, ffn_args, final):
    B, S, D = h.shape
    nt = B * S // TS
    d_ff = ffn_args[1].shape[1]
    last = nt - 1
    tile_in = lambda n: (jnp.minimum(n, last), 0, 0)
    tile_out = lambda n: (jnp.maximum(n - 1, 0), 0, 0)
    ffn_specs = [_resident((1, D)), _resident((D, d_ff)), _resident((D, d_ff)), _resident((d_ff, D)),
                 _resident((1, D))]
    mix_specs = [pl.BlockSpec((1, TS, 1), tile_in) if s == "pos" else s for s in mix_specs]
    out = pl.pallas_call(
        functools.partial(_layer_kernel, kind=kind, n_mix_in=len(mix_args), n_mix_scratch=len(mix_scratch),
                          tiles_per_seq=S // TS, final=final, chunk_decay=_chunk_decay()),
        grid=(nt + 1,),
        in_specs=[pl.BlockSpec((1, TS, D), tile_in)] + mix_specs + ffn_specs,
        out_specs=pl.BlockSpec((1, TS, D), tile_out),
        out_shape=jax.ShapeDtypeStruct((nt, TS, D), h.dtype),
        scratch_shapes=[pltpu.VMEM((TS, D), f32), pltpu.VMEM((TS, D), bf16), pltpu.VMEM((TS, d_ff), bf16)]
        + mix_scratch,
        compiler_params=pltpu.CompilerParams(dimension_semantics=("arbitrary",),
                                             vmem_limit_bytes=VMEM_LIMIT_BYTES),
        name=kind + ("_layer_final" if final else "_layer"),
    )(h.reshape(nt, TS, D), *mix_args, *ffn_args)
    return out.reshape(B, S, D)


def _even_layer(h, positions, norm_g, w_in, ret_norm_g, pool_w, pool_scale, w_out, ffn_args, final):
    B, S, D = h.shape
    in_width = w_in.shape[1]
    half = HEAD_DIM // 2
    inv_freq = ROPE_BASE ** (-jnp.arange(half, dtype=f32) / half)
    inv_freq = jnp.concatenate([inv_freq, inv_freq])[None, :]
    sign = jnp.concatenate([-jnp.ones((half,), f32), jnp.ones((half,), f32)])[None, :]
    mask, q_decay, k_decay = _retention_constants()
    cst = (N_HEADS, CHUNK, HEAD_DIM)
    mix_args = (positions.reshape(B * S // TS, TS, 1), norm_g, w_in, inv_freq, sign, mask, q_decay, k_decay,
                ret_norm_g, pool_w, pool_scale, w_out)
    mix_specs = ["pos", _resident((1, D)), _resident((D, in_width)),
                 _resident((1, HEAD_DIM)), _resident((1, HEAD_DIM)),
                 _resident(cst), _resident(cst), _resident(cst),
                 _resident((1, RET_WIDTH)),
                 _resident((len(POOL_WINDOWS), POOL_GROUP, POOL_GROUP)),
                 _resident((1, POOL_WIDTH)),
                 _resident((RET_WIDTH + POOL_WIDTH, D))]
    mix_scratch = [pltpu.VMEM((TS, D), bf16),
                   pltpu.VMEM((TS, in_width), f32),
                   pltpu.VMEM((N_HEADS, HEAD_DIM, HEAD_DIM), f32),
                   pltpu.VMEM((POOL_HALO + TS, POOL_WIDTH), f32),
                   pltpu.VMEM((TS, RET_WIDTH + POOL_WIDTH), bf16)]
    return _layer("even", h, mix_args, mix_specs, mix_scratch, ffn_args, final)


def _odd_layer(h, norm_g, w1, b1, w_dw, b_dw, ln_g, ln_b, w2, b2, ffn_args, final):
    B, S, D = h.shape
    mix_args = (norm_g, w1, b1, w_dw, b_dw, ln_g, ln_b, w2, b2)
    mix_specs = [_resident((1, D)), _resident((D, 2 * D)), _resident((1, 2 * D)),
                 _resident((CONV_WIDTH, D)), _resident((1, D)), _resident((1, D)), _resident((1, D)),
                 _resident((D, D)), _resident((1, D))]
    mix_scratch = [pltpu.VMEM((TS, D), bf16),
                   pltpu.VMEM((N_SLABS, SUBLANES * Z_PITCH, LANES), f32),
                   pltpu.VMEM((N_SLABS, (SEG + CONV_WIDTH - 1) * SUBLANES, LANES), f32),
                   pltpu.VMEM((N_SLABS, SUBLANES * Y_PITCH, LANES), f32),
                   pltpu.VMEM((N_SLABS, HALO, LANES), f32)]
    return _layer("odd", h, mix_args, mix_specs, mix_scratch, ffn_args, final)


def kernel(x, positions, mixer_norm_g, ffn_norm_g, final_norm_g, ret_w_in, ret_norm_g, pool_w, pool_scale, mix_w_out, conv_w_pw1, conv_b_pw1, conv_w_dw, conv_b_dw, conv_ln_g, conv_ln_b, conv_w_pw2, conv_b_pw2, ffn_w_gate, ffn_w_up, ffn_w_down):
    depth = mixer_norm_g.shape[0]
    row = lambda v: v[None, :]
    h = x
    for layer in range(depth):
        i = layer // 2
        final = layer == depth - 1
        ffn_args = (row(ffn_norm_g[layer]), ffn_w_gate[layer].astype(bf16), ffn_w_up[layer].astype(bf16),
                    ffn_w_down[layer].astype(bf16), row(final_norm_g))
        if layer % 2 == 0:
            h = _even_layer(h, positions, row(mixer_norm_g[layer]), ret_w_in[i].astype(bf16),
                            row(ret_norm_g[i]), pool_w[i].astype(bf16), row(pool_scale[i]),
                            mix_w_out[i].astype(bf16), ffn_args, final)
        else:
            h = _odd_layer(h, row(mixer_norm_g[layer]), conv_w_pw1[i].astype(bf16), row(conv_b_pw1[i]),
                           conv_w_dw[i], row(conv_b_dw[i]), row(conv_ln_g[i]), row(conv_ln_b[i]),
                           conv_w_pw2[i].astype(bf16), row(conv_b_pw2[i]), ffn_args, final)
    return h
```

```python
import functools

import numpy as np
import jax
import jax.numpy as jnp
from jax import lax
from jax.experimental import pallas as pl
from jax.experimental.pallas import tpu as pltpu

D_MODEL = 1024
N_HEADS = 4
HEAD_DIM = 128
RET_WIDTH = N_HEADS * HEAD_DIM
POOL_WINDOWS = (2, 4, 8, 16)
POOL_GROUP = 128
POOL_WIDTH = len(POOL_WINDOWS) * POOL_GROUP
CHUNK = 128
ROPE_BASE = 10000.0
CONV_WIDTH = 31
EPS = 1e-6

LANES = 128
SUBLANES = 8
MXU_COLS = 256
VMEM_LIMIT_BYTES = 56 * 1024 * 1024

TS = 512

SEG = TS // SUBLANES
HALO = 32
Z_PITCH = 104
Y_PITCH = 72
N_SLABS = D_MODEL // LANES
CONV_PB = 4

POOL_HALO = 16

bf16 = jnp.bfloat16
f32 = jnp.float32


def _dot(a, b):
    return jnp.dot(a, b, preferred_element_type=f32)


def _sigmoid(x):
    return 1.0 / (1.0 + jnp.exp(-x))


def _rms_norm(x, g):
    ms = jnp.mean(x * x, axis=-1, keepdims=True)
    return x * lax.rsqrt(ms + EPS) * g


def _ffn_stream(h1_ref, g_ref, wg_ref, wu_ref, wd_ref, fg_ref, hn_ref, act_ref, o_ref, final):
    x = h1_ref[...]
    hn_ref[...] = _rms_norm(x, g_ref[...]).astype(bf16)
    o_ref[0] = x
    yield
    d_ff = wg_ref.shape[1]
    for c in range(d_ff // MXU_COLS):
        cols = pl.ds(c * MXU_COLS, MXU_COLS)
        gate = _dot(hn_ref[...], wg_ref[:, cols])
        up = _dot(hn_ref[...], wu_ref[:, cols])
        act_ref[:, cols] = (gate * _sigmoid(gate) * up).astype(bf16)
        yield
    for c in range(D_MODEL // MXU_COLS):
        cols = pl.ds(c * MXU_COLS, MXU_COLS)
        o_ref[0, :, cols] = o_ref[0, :, cols] + _dot(act_ref[...], wd_ref[:, cols])
        yield
    if final:
        o_ref[0] = _rms_norm(o_ref[0], fg_ref[...])


def _even_init(first, state_ref, uext_ref):
    @pl.when(first)
    def _():
        state_ref[...] = jnp.zeros_like(state_ref)
        uext_ref[pl.ds(0, POOL_HALO), :] = jnp.zeros((POOL_HALO, POOL_WIDTH), f32)


def _even_stream(x_ref, h1_ref, tile_in_seq, pos_ref, ng_ref, win_ref, invf_ref, sign_ref, mask_ref,
                 qdec_ref, kdec_ref, rng_ref, pw_ref, ps_ref, wout_ref,
                 hn_ref, proj_ref, state_ref, uext_ref, mix_ref, chunk_decay):
    hn_ref[...] = _rms_norm(x_ref[0], ng_ref[...]).astype(bf16)
    in_width = win_ref.shape[1]
    for part in range(in_width // RET_WIDTH):
        cols = pl.ds(part * RET_WIDTH, RET_WIDTH)
        proj_ref[:, cols] = _dot(hn_ref[...], win_ref[:, cols])
        yield

    ang = pos_ref[0].astype(f32) * invf_ref[...]
    cos_t = jnp.cos(ang)
    sin_t = jnp.sin(ang) * sign_ref[...]

    def rope(t, cs, sn):
        return t * cs + pltpu.roll(t, HEAD_DIM // 2, axis=1) * sn

    heads = range(N_HEADS)
    for c in range(TS // CHUNK):
        rows = pl.ds(c * CHUNK, CHUNK)
        cs = cos_t[c * CHUNK:(c + 1) * CHUNK]
        sn = sin_t[c * CHUNK:(c + 1) * CHUNK]
        q = [rope(proj_ref[rows, pl.ds(h * HEAD_DIM, HEAD_DIM)], cs, sn) for h in heads]
        k = [rope(proj_ref[rows, pl.ds(RET_WIDTH + h * HEAD_DIM, HEAD_DIM)], cs, sn) for h in heads]
        v = [proj_ref[rows, pl.ds(2 * RET_WIDTH + h * HEAD_DIM, HEAD_DIM)].astype(bf16) for h in heads]
        scores = [lax.dot_general(q[h].astype(bf16), k[h].astype(bf16), (((1,), (1,)), ((), ())),
                                  preferred_element_type=f32) for h in heads]
        yield
        o = []
        for h in heads:
            sm = (scores[h] * mask_ref[h]).astype(bf16)
            qd = (q[h] * qdec_ref[h]).astype(bf16)
            st = state_ref[h]
            o.append(_dot(jnp.concatenate([sm, qd], axis=1),
                          jnp.concatenate([v[h], st.astype(bf16)], axis=0)))
            kd = (k[h] * kdec_ref[h]).astype(bf16)
            kv = lax.dot_general(kd, v[h], (((0,), (0,)), ((), ())), preferred_element_type=f32)
            state_ref[h] = st * chunk_decay[h] + kv
        yield
        for h in heads:
            col = pl.ds(h * HEAD_DIM, HEAD_DIM)
            gate = proj_ref[rows, pl.ds(3 * RET_WIDTH + h * HEAD_DIM, HEAD_DIM)]
            mu = jnp.mean(o[h], axis=-1, keepdims=True)
            oc = o[h] - mu
            var = jnp.mean(oc * oc, axis=-1, keepdims=True)
            on = oc * lax.rsqrt(var + EPS) * rng_ref[:, col]
            mix_ref[rows, col] = (gate * _sigmoid(gate) * on).astype(bf16)

    ucols = pl.ds(4 * RET_WIDTH, POOL_WIDTH)
    uext_ref[pl.ds(POOL_HALO, TS), :] = proj_ref[:, ucols]
    row = lax.broadcasted_iota(jnp.int32, (TS, POOL_GROUP), 0) + tile_in_seq * TS + 1
    for gi, w in enumerate(POOL_WINDOWS):
        gcols = pl.ds(gi * POOL_GROUP, POOL_GROUP)
        e = uext_ref[:, gcols]
        s = e
        step = 1
        while step < w:
            s = s + pltpu.roll(s, step, axis=0)
            step *= 2
        cnt = jnp.minimum(row, w).astype(f32)
        y = s[POOL_HALO:] / cnt - e[POOL_HALO:]
        pooled = _dot(y.astype(bf16), pw_ref[gi]) * ps_ref[:, gcols]
        mix_ref[:, pl.ds(RET_WIDTH + gi * POOL_GROUP, POOL_GROUP)] = pooled.astype(bf16)
    uext_ref[pl.ds(0, POOL_HALO), :] = uext_ref[pl.ds(TS, POOL_HALO), :]
    yield

    yield TAIL
    for c in range(D_MODEL // MXU_COLS):
        cols = pl.ds(c * MXU_COLS, MXU_COLS)
        h1_ref[:, cols] = x_ref[0, :, cols] + _dot(mix_ref[...], wout_ref[:, cols])
        yield


def _retention_constants():
    idx = jnp.arange(CHUNK, dtype=f32)
    log_gamma = jnp.log1p(-(2.0 ** (-5.0 - jnp.arange(N_HEADS, dtype=f32))))
    rel = idx[:, None] - idx[None, :]
    mask = jnp.where(rel >= 0, jnp.exp(log_gamma[:, None, None] * jnp.maximum(rel, 0.0)), 0.0)
    scale = HEAD_DIM ** -0.5
    q_decay = jnp.exp(log_gamma[:, None] * (idx + 1.0))
    k_decay = jnp.exp(log_gamma[:, None] * (CHUNK - 1.0 - idx))
    ones = jnp.ones((1, 1, HEAD_DIM), f32)
    return (mask * scale, q_decay[:, :, None] * ones, (k_decay * scale)[:, :, None] * ones)


def _chunk_decay():
    lg = np.log1p(-(2.0 ** (-5.0 - np.arange(N_HEADS, dtype=np.float32)))).astype(np.float32)
    return tuple(float(v) for v in np.exp(lg * np.float32(CHUNK)).astype(np.float32))


def _odd_init(first, carry_ref):
    @pl.when(first)
    def _():
        carry_ref[...] = jnp.zeros_like(carry_ref)


def _odd_stream(x_ref, h1_ref, ng_ref, w1_ref, b1_ref, wdw_ref, bdw_ref, lng_ref, lnb_ref, w2_ref, b2_ref,
                hn_ref, z_ref, e_ref, y_ref, carry_ref):
    hn_ref[...] = _rms_norm(x_ref[0], ng_ref[...]).astype(bf16)
    base = HALO - (CONV_WIDTH - 1)
    n_e = SEG + CONV_WIDTH - 1
    per_blk = MXU_COLS // LANES

    for c2 in range(D_MODEL // MXU_COLS):
        cols = pl.ds(c2 * MXU_COLS, MXU_COLS)
        gcols = pl.ds(D_MODEL + c2 * MXU_COLS, MXU_COLS)
        a = _dot(hn_ref[...], w1_ref[:, cols]) + b1_ref[:, cols]
        gt = _dot(hn_ref[...], w1_ref[:, gcols]) + b1_ref[:, gcols]
        yield
        u2 = a * _sigmoid(gt)
        for ci in range(per_blk):
            c = c2 * per_blk + ci
            uc = u2[:, ci * LANES:(ci + 1) * LANES]
            z_ref[c, pl.ds(0, HALO), :] = carry_ref[c]
            for s in range(SUBLANES):
                z_ref[c, pl.ds(s * Z_PITCH + HALO, SEG), :] = uc[s * SEG:(s + 1) * SEG]
                tail = uc[(s + 1) * SEG - HALO:(s + 1) * SEG]
                if s + 1 < SUBLANES:
                    z_ref[c, pl.ds((s + 1) * Z_PITCH, HALO), :] = tail
                else:
                    carry_ref[c] = tail
            for m in range(n_e):
                e_ref[c, pl.ds(m * SUBLANES, SUBLANES), :] = z_ref[c, pl.ds(base + m, SUBLANES, stride=Z_PITCH), :]
            lanes = pl.ds(c * LANES, LANES)
            wv = [jnp.broadcast_to(wdw_ref[pl.ds(k, 1), lanes], (SUBLANES, LANES)) for k in range(CONV_WIDTH)]
            bias = jnp.broadcast_to(bdw_ref[:, lanes], (SUBLANES, LANES))
            for p0 in range(0, SEG, CONV_PB):
                ev = [e_ref[c, pl.ds((p0 + m) * SUBLANES, SUBLANES), :] for m in range(CONV_PB + CONV_WIDTH - 1)]
                for i in range(CONV_PB):
                    acc = bias
                    for k in range(CONV_WIDTH):
                        acc = acc + ev[i + k] * wv[k]
                    y_ref[c, pl.ds(p0 + i, SUBLANES, stride=Y_PITCH), :] = acc

    d = jnp.concatenate(
        [jnp.concatenate([y_ref[c, pl.ds(s * Y_PITCH, SEG), :] for c in range(N_SLABS)], axis=1)
         for s in range(SUBLANES)], axis=0)
    mu = jnp.mean(d, axis=-1, keepdims=True)
    dc = d - mu
    var = jnp.mean(dc * dc, axis=-1, keepdims=True)
    zz = dc * lax.rsqrt(var + EPS) * lng_ref[...] + lnb_ref[...]
    hn_ref[...] = (zz * _sigmoid(zz)).astype(bf16)
    yield TAIL
    for c in range(D_MODEL // MXU_COLS):
        cols = pl.ds(c * MXU_COLS, MXU_COLS)
        h1_ref[:, cols] = x_ref[0, :, cols] + _dot(hn_ref[...], w2_ref[:, cols]) + b2_ref[:, cols]
        yield


N_FFN_IN = 5
TAIL = "tail"


def _interleave(ffn, n_ffn_paced, mixer, n_mixer_head):
    emitted = 0
    for i in range(n_mixer_head):
        next(mixer)
        target = (i + 1) * n_ffn_paced // n_mixer_head
        while emitted < target:
            next(ffn)
            emitted += 1
    assert next(mixer) is TAIL
    for _ in ffn:
        pass
    for _ in mixer:
        pass


def _layer_kernel(*refs, kind, n_mix_in, n_mix_scratch, tiles_per_seq, final, chunk_decay):
    x_ref = refs[0]
    mix_in = refs[1:1 + n_mix_in]
    ffn_in = refs[1 + n_mix_in:1 + n_mix_in + N_FFN_IN]
    o_ref = refs[1 + n_mix_in + N_FFN_IN]
    scratch = refs[2 + n_mix_in + N_FFN_IN:]
    h1_ref, hnf_ref, act_ref = scratch[:3]
    mix_scratch = scratch[3:3 + n_mix_scratch]

    n = pl.program_id(0)
    tile_in_seq = lax.rem(n, tiles_per_seq)
    first = tile_in_seq == 0

    @pl.when(n == 0)
    def _():
        h1_ref[...] = jnp.zeros((TS, D_MODEL), f32)

    n_gate_up = ffn_in[1].shape[1] // MXU_COLS
    if kind == "even":
        _even_init(first, mix_scratch[2], mix_scratch[3])
        mixer = _even_stream(x_ref, h1_ref, tile_in_seq, *mix_in, *mix_scratch, chunk_decay)
        n_mixer_head = mix_in[2].shape[1] // RET_WIDTH + 2 * (TS // CHUNK) + 1
        n_ffn_paced = n_mixer_head
    else:
        _odd_init(first, mix_scratch[4])
        mixer = _odd_stream(x_ref, h1_ref, *mix_in, *mix_scratch)
        n_mixer_head = D_MODEL // MXU_COLS
        n_ffn_paced = n_gate_up

    ffn = _ffn_stream(h1_ref, *ffn_in, hnf_ref, act_ref, o_ref, final)
    next(ffn)
    _interleave(ffn, n_ffn_paced, mixer, n_mixer_head)


def _resident(shape):
    zeros = (0,) * len(shape)
    return pl.BlockSpec(shape, lambda n: zeros, pipeline_mode=pl.Buffered(1))


def _layer(kind, h, mix_args, mix_specs, mix_scratch, ffn_args, final):
    B, S, D = h.shape
    nt = B * S // TS
    d_ff = ffn_args[1].shape[1]
    last = nt - 1
    tile_in = lambda n: (jnp.minimum(n, last), 0, 0)
    tile_out = lambda n: (jnp.maximum(n - 1, 0), 0, 0)
    ffn_specs = [_resident((1, D)), _resident((D, d_ff)), _resident((D, d_ff)), _resident((d_ff, D)),
                 _resident((1, D))]
    mix_specs = [pl.BlockSpec((1, TS, HEAD_DIM), tile_in) if s == "pos" else s for s in mix_specs]
    out = pl.pallas_call(
        functools.partial(_layer_kernel, kind=kind, n_mix_in=len(mix_args), n_mix_scratch=len(mix_scratch),
                          tiles_per_seq=S // TS, final=final, chunk_decay=_chunk_decay()),
        grid=(nt + 1,),
        in_specs=[pl.BlockSpec((1, TS, D), tile_in)] + mix_specs + ffn_specs,
        out_specs=pl.BlockSpec((1, TS, D), tile_out),
        out_shape=jax.ShapeDtypeStruct((nt, TS, D), h.dtype),
        scratch_shapes=[pltpu.VMEM((TS, D), f32), pltpu.VMEM((TS, D), bf16), pltpu.VMEM((TS, d_ff), bf16)]
        + mix_scratch,
        compiler_params=pltpu.CompilerParams(dimension_semantics=("arbitrary",),
                                             vmem_limit_bytes=VMEM_LIMIT_BYTES),
        name=kind + ("_layer_final" if final else "_layer"),
    )(h.reshape(nt, TS, D), *mix_args, *ffn_args)
    return out.reshape(B, S, D)


def _even_layer(h, positions, norm_g, w_in, ret_norm_g, pool_w, pool_scale, w_out, ffn_args, final):
    B, S, D = h.shape
    in_width = w_in.shape[1]
    half = HEAD_DIM // 2
    inv_freq = ROPE_BASE ** (-jnp.arange(half, dtype=f32) / half)
    inv_freq = jnp.concatenate([inv_freq, inv_freq])[None, :]
    sign = jnp.concatenate([-jnp.ones((half,), f32), jnp.ones((half,), f32)])[None, :]
    mask, q_decay, k_decay = _retention_constants()
    cst = (N_HEADS, CHUNK, HEAD_DIM)
    pos = jnp.broadcast_to(positions.reshape(B * S // TS, TS, 1), (B * S // TS, TS, HEAD_DIM))
    mix_args = (pos, norm_g, w_in, inv_freq, sign, mask, q_decay, k_decay,
                ret_norm_g, pool_w, pool_scale, w_out)
    mix_specs = ["pos", _resident((1, D)), _resident((D, in_width)),
                 _resident((1, HEAD_DIM)), _resident((1, HEAD_DIM)),
                 _resident(cst), _resident(cst), _resident(cst),
                 _resident((1, RET_WIDTH)),
                 _resident((len(POOL_WINDOWS), POOL_GROUP, POOL_GROUP)),
                 _resident((1, POOL_WIDTH)),
                 _resident((RET_WIDTH + POOL_WIDTH, D))]
    mix_scratch = [pltpu.VMEM((TS, D), bf16),
                   pltpu.VMEM((TS, in_width), f32),
                   pltpu.VMEM((N_HEADS, HEAD_DIM, HEAD_DIM), f32),
                   pltpu.VMEM((POOL_HALO + TS, POOL_WIDTH), f32),
                   pltpu.VMEM((TS, RET_WIDTH + POOL_WIDTH), bf16)]
    return _layer("even", h, mix_args, mix_specs, mix_scratch, ffn_args, final)


def _odd_layer(h, norm_g, w1, b1, w_dw, b_dw, ln_g, ln_b, w2, b2, ffn_args, final):
    B, S, D = h.shape
    mix_args = (norm_g, w1, b1, w_dw, b_dw, ln_g, ln_b, w2, b2)
    mix_specs = [_resident((1, D)), _resident((D, 2 * D)), _resident((1, 2 * D)),
                 _resident((CONV_WIDTH, D)), _resident((1, D)), _resident((1, D)), _resident((1, D)),
                 _resident((D, D)), _resident((1, D))]
    mix_scratch = [pltpu.VMEM((TS, D), bf16),
                   pltpu.VMEM((N_SLABS, SUBLANES * Z_PITCH, LANES), f32),
                   pltpu.VMEM((N_SLABS, (SEG + CONV_WIDTH - 1) * SUBLANES, LANES), f32),
                   pltpu.VMEM((N_SLABS, SUBLANES * Y_PITCH, LANES), f32),
                   pltpu.VMEM((N_SLABS, HALO, LANES), f32)]
    return _layer("odd", h, mix_args, mix_specs, mix_scratch, ffn_args, final)


def kernel(x, positions, mixer_norm_g, ffn_norm_g, final_norm_g, ret_w_in, ret_norm_g, pool_w, pool_scale, mix_w_out, conv_w_pw1, conv_b_pw1, conv_w_dw, conv_b_dw, conv_ln_g, conv_ln_b, conv_w_pw2, conv_b_pw2, ffn_w_gate, ffn_w_up, ffn_w_down):
    depth = mixer_norm_g.shape[0]
    row = lambda v: v[None, :]
    h = x
    for layer in range(depth):
        i = layer // 2
        final = layer == depth - 1
        ffn_args = (row(ffn_norm_g[layer]), ffn_w_gate[layer].astype(bf16), ffn_w_up[layer].astype(bf16),
                    ffn_w_down[layer].astype(bf16), row(final_norm_g))
        if layer % 2 == 0:
            h = _even_layer(h, positions, row(mixer_norm_g[layer]), ret_w_in[i].astype(bf16),
                            row(ret_norm_g[i]), pool_w[i].astype(bf16), row(pool_scale[i]),
                            mix_w_out[i].astype(bf16), ffn_args, final)
        else:
            h = _odd_layer(h, row(mixer_norm_g[layer]), conv_w_pw1[i].astype(bf16), row(conv_b_pw1[i]),
                           conv_w_dw[i], row(conv_b_dw[i]), row(conv_ln_g[i]), row(conv_ln_b[i]),
                           conv_w_pw2[i].astype(bf16), row(conv_b_pw2[i]), ffn_args, final)
    return h
```

```python
import functools

import numpy as np
import jax
import jax.numpy as jnp
from jax import lax
from jax.experimental import pallas as pl
from jax.experimental.pallas import tpu as pltpu

D_MODEL = 1024
N_HEADS = 4
HEAD_DIM = 128
RET_WIDTH = N_HEADS * HEAD_DIM
POOL_WINDOWS = (2, 4, 8, 16)
POOL_GROUP = 128
POOL_WIDTH = len(POOL_WINDOWS) * POOL_GROUP
CHUNK = 128
ROPE_BASE = 10000.0
CONV_WIDTH = 31
EPS = 1e-6

LANES = 128
SUBLANES = 8
MXU_COLS = 256
VMEM_LIMIT_BYTES = 56 * 1024 * 1024

TS = 512

SEG = TS // SUBLANES
HALO = 32
Z_PITCH = 104
Y_PITCH = 72
N_SLABS = D_MODEL // LANES
CONV_PB = 4

POOL_HALO = 16

bf16 = jnp.bfloat16
f32 = jnp.float32


def _dot(a, b):
    return jnp.dot(a, b, preferred_element_type=f32)


def _swish(x):
    h = 0.5 * x
    return h + h * jnp.tanh(h)


def _glu(a, g):
    h = 0.5 * a
    return h + h * jnp.tanh(0.5 * g)


def _rms_norm(x, g):
    ms = jnp.mean(x * x, axis=-1, keepdims=True)
    return x * lax.rsqrt(ms + EPS) * g


def _ffn_stream(h1_ref, g_ref, wg_ref, wu_ref, wd_ref, fg_ref, hn_ref, act_ref, o_ref, final):
    x = h1_ref[...]
    hn_ref[...] = _rms_norm(x, g_ref[...]).astype(bf16)
    o_ref[0] = x
    yield
    d_ff = wg_ref.shape[1]
    for c in range(d_ff // MXU_COLS):
        cols = pl.ds(c * MXU_COLS, MXU_COLS)
        gate = _dot(hn_ref[...], wg_ref[:, cols])
        up = _dot(hn_ref[...], wu_ref[:, cols])
        act_ref[:, cols] = (_swish(gate) * up).astype(bf16)
        yield
    for c in range(D_MODEL // MXU_COLS):
        cols = pl.ds(c * MXU_COLS, MXU_COLS)
        o_ref[0, :, cols] = o_ref[0, :, cols] + _dot(act_ref[...], wd_ref[:, cols])
        yield
    if final:
        o_ref[0] = _rms_norm(o_ref[0], fg_ref[...])


def _even_init(first, state_ref, uext_ref):
    @pl.when(first)
    def _():
        state_ref[...] = jnp.zeros_like(state_ref)
        uext_ref[pl.ds(0, POOL_HALO), :] = jnp.zeros((POOL_HALO, POOL_WIDTH), f32)


def _even_stream(x_ref, h1_ref, tile_in_seq, pos_ref, ng_ref, win_ref, invf_ref, sign_ref, mask_ref,
                 qdec_ref, kdec_ref, rng_ref, pw_ref, ps_ref, wout_ref,
                 hn_ref, proj_ref, state_ref, uext_ref, mix_ref, chunk_decay):
    hn_ref[...] = _rms_norm(x_ref[0], ng_ref[...]).astype(bf16)
    in_width = win_ref.shape[1]
    for part in range(in_width // RET_WIDTH):
        cols = pl.ds(part * RET_WIDTH, RET_WIDTH)
        proj_ref[:, cols] = _dot(hn_ref[...], win_ref[:, cols])
        yield

    ang = pos_ref[0].astype(f32) * invf_ref[...]
    cos_t = jnp.cos(ang)
    sin_t = jnp.sin(ang) * sign_ref[...]

    def rope(t, cs, sn):
        return t * cs + pltpu.roll(t, HEAD_DIM // 2, axis=1) * sn

    heads = range(N_HEADS)
    for c in range(TS // CHUNK):
        rows = pl.ds(c * CHUNK, CHUNK)
        cs = cos_t[c * CHUNK:(c + 1) * CHUNK]
        sn = sin_t[c * CHUNK:(c + 1) * CHUNK]
        q = [rope(proj_ref[rows, pl.ds(h * HEAD_DIM, HEAD_DIM)], cs, sn) for h in heads]
        k = [rope(proj_ref[rows, pl.ds(RET_WIDTH + h * HEAD_DIM, HEAD_DIM)], cs, sn) for h in heads]
        v = [proj_ref[rows, pl.ds(2 * RET_WIDTH + h * HEAD_DIM, HEAD_DIM)].astype(bf16) for h in heads]
        scores = [lax.dot_general(q[h].astype(bf16), k[h].astype(bf16), (((1,), (1,)), ((), ())),
                                  preferred_element_type=f32) for h in heads]
        yield
        o = []
        for h in heads:
            sm = (scores[h] * mask_ref[h]).astype(bf16)
            qd = (q[h] * qdec_ref[h]).astype(bf16)
            st = state_ref[h]
            o.append(_dot(jnp.concatenate([sm, qd], axis=1),
                          jnp.concatenate([v[h], st.astype(bf16)], axis=0)))
            kd = (k[h] * kdec_ref[h]).astype(bf16)
            kv = lax.dot_general(kd, v[h], (((0,), (0,)), ((), ())), preferred_element_type=f32)
            state_ref[h] = st * chunk_decay[h] + kv
        yield
        for h in heads:
            col = pl.ds(h * HEAD_DIM, HEAD_DIM)
            gate = proj_ref[rows, pl.ds(3 * RET_WIDTH + h * HEAD_DIM, HEAD_DIM)]
            mu = jnp.mean(o[h], axis=-1, keepdims=True)
            oc = o[h] - mu
            var = jnp.mean(oc * oc, axis=-1, keepdims=True)
            on = oc * lax.rsqrt(var + EPS) * rng_ref[:, col]
            mix_ref[rows, col] = (_swish(gate) * on).astype(bf16)

    ucols = pl.ds(4 * RET_WIDTH, POOL_WIDTH)
    uext_ref[pl.ds(POOL_HALO, TS), :] = proj_ref[:, ucols]
    row = lax.broadcasted_iota(jnp.int32, (TS, POOL_GROUP), 0) + tile_in_seq * TS + 1
    for gi, w in enumerate(POOL_WINDOWS):
        gcols = pl.ds(gi * POOL_GROUP, POOL_GROUP)
        e = uext_ref[:, gcols]
        s = e
        step = 1
        while step < w:
            s = s + pltpu.roll(s, step, axis=0)
            step *= 2
        cnt = jnp.minimum(row, w).astype(f32)
        y = s[POOL_HALO:] / cnt - e[POOL_HALO:]
        pooled = _dot(y.astype(bf16), pw_ref[gi]) * ps_ref[:, gcols]
        mix_ref[:, pl.ds(RET_WIDTH + gi * POOL_GROUP, POOL_GROUP)] = pooled.astype(bf16)
    uext_ref[pl.ds(0, POOL_HALO), :] = uext_ref[pl.ds(TS, POOL_HALO), :]
    yield

    for c in range(D_MODEL // MXU_COLS):
        cols = pl.ds(c * MXU_COLS, MXU_COLS)
        h1_ref[:, cols] = x_ref[0, :, cols] + _dot(mix_ref[...], wout_ref[:, cols])
        yield


def _retention_constants():
    idx = jnp.arange(CHUNK, dtype=f32)
    log_gamma = jnp.log1p(-(2.0 ** (-5.0 - jnp.arange(N_HEADS, dtype=f32))))
    rel = idx[:, None] - idx[None, :]
    mask = jnp.where(rel >= 0, jnp.exp(log_gamma[:, None, None] * jnp.maximum(rel, 0.0)), 0.0)
    scale = HEAD_DIM ** -0.5
    q_decay = jnp.exp(log_gamma[:, None] * (idx + 1.0))
    k_decay = jnp.exp(log_gamma[:, None] * (CHUNK - 1.0 - idx))
    ones = jnp.ones((1, 1, HEAD_DIM), f32)
    return (mask * scale, q_decay[:, :, None] * ones, (k_decay * scale)[:, :, None] * ones)


def _chunk_decay():
    lg = np.log1p(-(2.0 ** (-5.0 - np.arange(N_HEADS, dtype=np.float32)))).astype(np.float32)
    return tuple(float(v) for v in np.exp(lg * np.float32(CHUNK)).astype(np.float32))


def _odd_init(first, carry_ref):
    @pl.when(first)
    def _():
        carry_ref[...] = jnp.zeros_like(carry_ref)


def _odd_conv_stream(x_ref, ng_ref, w1_ref, b1_ref, wdw_ref, bdw_ref, hn_ref, z_ref, e_ref, y_ref, carry_ref):
    hn_ref[...] = _rms_norm(x_ref[0], ng_ref[...]).astype(bf16)
    base = HALO - (CONV_WIDTH - 1)
    n_e = SEG + CONV_WIDTH - 1
    per_blk = MXU_COLS // LANES
    for c2 in range(D_MODEL // MXU_COLS):
        cols = pl.ds(c2 * MXU_COLS, MXU_COLS)
        gcols = pl.ds(D_MODEL + c2 * MXU_COLS, MXU_COLS)
        a = _dot(hn_ref[...], w1_ref[:, cols]) + b1_ref[:, cols]
        gt = _dot(hn_ref[...], w1_ref[:, gcols]) + b1_ref[:, gcols]
        yield
        u2 = _glu(a, gt)
        for ci in range(per_blk):
            c = c2 * per_blk + ci
            uc = u2[:, ci * LANES:(ci + 1) * LANES]
            z_ref[c, pl.ds(0, HALO), :] = carry_ref[c]
            for s in range(SUBLANES):
                z_ref[c, pl.ds(s * Z_PITCH + HALO, SEG), :] = uc[s * SEG:(s + 1) * SEG]
                tail = uc[(s + 1) * SEG - HALO:(s + 1) * SEG]
                if s + 1 < SUBLANES:
                    z_ref[c, pl.ds((s + 1) * Z_PITCH, HALO), :] = tail
                else:
                    carry_ref[c] = tail
            for m in range(n_e):
                e_ref[c, pl.ds(m * SUBLANES, SUBLANES), :] = z_ref[c, pl.ds(base + m, SUBLANES, stride=Z_PITCH), :]
            lanes = pl.ds(c * LANES, LANES)
            wv = [jnp.broadcast_to(wdw_ref[pl.ds(k, 1), lanes], (SUBLANES, LANES)) for k in range(CONV_WIDTH)]
            bias = jnp.broadcast_to(bdw_ref[:, lanes], (SUBLANES, LANES))
            for p0 in range(0, SEG, CONV_PB):
                ev = [e_ref[c, pl.ds((p0 + m) * SUBLANES, SUBLANES), :] for m in range(CONV_PB + CONV_WIDTH - 1)]
                for i in range(CONV_PB):
                    acc = bias
                    for k in range(CONV_WIDTH):
                        acc = acc + ev[i + k] * wv[k]
                    y_ref[c, pl.ds(p0 + i, SUBLANES, stride=Y_PITCH), :] = acc
    yield


def _odd_out_stream(res_ref, h1_ref, lng_ref, lnb_ref, w2_ref, b2_ref, hn_ref, y_ref):
    d = jnp.concatenate(
        [jnp.concatenate([y_ref[c, pl.ds(s * Y_PITCH, SEG), :] for c in range(N_SLABS)], axis=1)
         for s in range(SUBLANES)], axis=0)
    mu = jnp.mean(d, axis=-1, keepdims=True)
    dc = d - mu
    var = jnp.mean(dc * dc, axis=-1, keepdims=True)
    zz = dc * lax.rsqrt(var + EPS) * lng_ref[...] + lnb_ref[...]
    hn_ref[...] = _swish(zz).astype(bf16)
    yield
    for c in range(D_MODEL // MXU_COLS):
        cols = pl.ds(c * MXU_COLS, MXU_COLS)
        h1_ref[:, cols] = res_ref[:, cols] + _dot(hn_ref[...], w2_ref[:, cols]) + b2_ref[:, cols]
        yield


N_FFN_IN = 5


def _emit(order, streams):
    for key in order:
        next(streams[key])
    for gen in streams.values():
        for _ in gen:
            pass


def _split_refs(refs, n_mix_in):
    x_ref = refs[0]
    mix_in = refs[1:1 + n_mix_in]
    ffn_in = refs[1 + n_mix_in:1 + n_mix_in + N_FFN_IN]
    o_ref = refs[1 + n_mix_in + N_FFN_IN]
    scratch = refs[2 + n_mix_in + N_FFN_IN:]
    return x_ref, mix_in, ffn_in, o_ref, scratch[:3], scratch[3:]


def _even_kernel(*refs, n_mix_in, tiles_per_seq, final, chunk_decay):
    x_ref, mix_in, ffn_in, o_ref, (h1_ref, hnf_ref, act_ref), mix_scratch = _split_refs(refs, n_mix_in)
    n = pl.program_id(0)
    tile_in_seq = lax.rem(n, tiles_per_seq)

    @pl.when(n == 0)
    def _():
        h1_ref[...] = jnp.zeros((TS, D_MODEL), f32)

    _even_init(tile_in_seq == 0, mix_scratch[2], mix_scratch[3])
    ffn = _ffn_stream(h1_ref, *ffn_in, hnf_ref, act_ref, o_ref, final)
    mixer = _even_stream(x_ref, h1_ref, tile_in_seq, *mix_in, *mix_scratch, chunk_decay)
    next(ffn)
    n_ffn = ffn_in[1].shape[1] // MXU_COLS + D_MODEL // MXU_COLS
    n_head = mix_in[2].shape[1] // RET_WIDTH + 2 * (TS // CHUNK) + 1
    n_tail = D_MODEL // MXU_COLS
    _emit("MF" * n_head + "F" * (n_ffn - n_head) + "M" * n_tail, {"F": ffn, "M": mixer})


def _odd_kernel(*refs, n_mix_in, tiles_per_seq, final):
    x_ref, mix_in, ffn_in, o_ref, (h1_ref, hnf_ref, act_ref), mix_scratch = _split_refs(refs, n_mix_in)
    ng_ref, w1_ref, b1_ref, wdw_ref, bdw_ref, lng_ref, lnb_ref, w2_ref, b2_ref = mix_in
    hn_ref, z_ref, e_ref, y_ref, carry_ref, hn2_ref = mix_scratch
    n = pl.program_id(0)

    @pl.when(n == 0)
    def _():
        h1_ref[...] = jnp.zeros((TS, D_MODEL), f32)

    _odd_init(lax.rem(n, tiles_per_seq) == 0, carry_ref)
    ffn = _ffn_stream(h1_ref, *ffn_in, hnf_ref, act_ref, o_ref, final)
    conv = _odd_conv_stream(x_ref, ng_ref, w1_ref, b1_ref, wdw_ref, bdw_ref, hn_ref, z_ref, e_ref, y_ref,
                            carry_ref)
    out = _odd_out_stream(x_ref.at[0], h1_ref, lng_ref, lnb_ref, w2_ref, b2_ref, hn2_ref, y_ref)
    next(ffn)
    n_gate_up = ffn_in[1].shape[1] // MXU_COLS
    n_blk = D_MODEL // MXU_COLS
    order = "CFF" + "CFFF" * (n_blk - 1) + "CO" + "F" * n_blk + "O" * n_blk
    assert (order.count("F"), order.count("C"), order.count("O")) == (n_gate_up + n_blk, n_blk + 1, n_blk + 1)
    _emit(order, {"F": ffn, "C": conv, "O": out})


def _whole(arr):
    zeros = (0,) * arr.ndim
    return arr, pl.BlockSpec(arr.shape, lambda n: zeros, pipeline_mode=pl.Buffered(1))


def _stacked(arr, idx):
    if arr.ndim == 2:
        arr = arr[:, None, :]
    zeros = (0,) * (arr.ndim - 1)
    return arr, pl.BlockSpec((None,) + arr.shape[1:], lambda n: (idx,) + zeros, pipeline_mode=pl.Buffered(1))


def _layer(kind, h, mix, mix_scratch, ffn, final):
    B, S, D = h.shape
    nt = B * S // TS
    d_ff = ffn[1][0].shape[-1]
    last = nt - 1
    tile_in = lambda n: (jnp.minimum(n, last), 0, 0)
    tile_out = lambda n: (jnp.maximum(n - 1, 0), 0, 0)
    if kind == "even":
        body = functools.partial(_even_kernel, n_mix_in=len(mix), tiles_per_seq=S // TS, final=final,
                                 chunk_decay=_chunk_decay())
    else:
        body = functools.partial(_odd_kernel, n_mix_in=len(mix), tiles_per_seq=S // TS, final=final)
    mix_specs = [pl.BlockSpec((1, TS, a.shape[-1]), tile_in) if s == "tile" else s for a, s in mix]
    out = pl.pallas_call(
        body,
        grid=(nt + 1,),
        in_specs=[pl.BlockSpec((1, TS, D), tile_in)] + mix_specs + [s for _, s in ffn],
        out_specs=pl.BlockSpec((1, TS, D), tile_out),
        out_shape=jax.ShapeDtypeStruct((nt, TS, D), h.dtype),
        scratch_shapes=[pltpu.VMEM((TS, D), f32), pltpu.VMEM((TS, D), bf16), pltpu.VMEM((TS, d_ff), bf16)]
        + mix_scratch,
        compiler_params=pltpu.CompilerParams(dimension_semantics=("arbitrary",),
                                             vmem_limit_bytes=VMEM_LIMIT_BYTES),
        name=kind + ("_layer_final" if final else "_layer"),
    )(h.reshape(nt, TS, D), *[a for a, _ in mix], *[a for a, _ in ffn])
    return out.reshape(B, S, D)


def _even_layer(h, pos, mix_params, ffn, final):
    B, S, D = h.shape
    in_width = mix_params[1][0].shape[-1]
    half = HEAD_DIM // 2
    inv_freq = ROPE_BASE ** (-jnp.arange(half, dtype=f32) / half)
    inv_freq = jnp.concatenate([inv_freq, inv_freq])[None, :]
    sign = jnp.concatenate([-jnp.ones((half,), f32), jnp.ones((half,), f32)])[None, :]
    norm_g, w_in, ret_norm_g, pool_w, pool_scale, w_out = mix_params
    mix = [(pos, "tile"), norm_g, w_in, _whole(inv_freq), _whole(sign)]
    mix += [_whole(c) for c in _retention_constants()]
    mix += [ret_norm_g, pool_w, pool_scale, w_out]
    mix_scratch = [pltpu.VMEM((TS, D), bf16),
                   pltpu.VMEM((TS, in_width), f32),
                   pltpu.VMEM((N_HEADS, HEAD_DIM, HEAD_DIM), f32),
                   pltpu.VMEM((POOL_HALO + TS, POOL_WIDTH), f32),
                   pltpu.VMEM((TS, RET_WIDTH + POOL_WIDTH), bf16)]
    return _layer("even", h, mix, mix_scratch, ffn, final)


def _odd_layer(h, mix_params, ffn, final):
    B, S, D = h.shape
    mix_scratch = [pltpu.VMEM((TS, D), bf16),
                   pltpu.VMEM((N_SLABS, SUBLANES * Z_PITCH, LANES), f32),
                   pltpu.VMEM((N_SLABS, (SEG + CONV_WIDTH - 1) * SUBLANES, LANES), f32),
                   pltpu.VMEM((N_SLABS, SUBLANES * Y_PITCH, LANES), f32),
                   pltpu.VMEM((N_SLABS, HALO, LANES), f32),
                   pltpu.VMEM((TS, D), bf16)]
    return _layer("odd", h, list(mix_params), mix_scratch, ffn, final)


def kernel(x, positions, mixer_norm_g, ffn_norm_g, final_norm_g, ret_w_in, ret_norm_g, pool_w, pool_scale, mix_w_out, conv_w_pw1, conv_b_pw1, conv_w_dw, conv_b_dw, conv_ln_g, conv_ln_b, conv_w_pw2, conv_b_pw2, ffn_w_gate, ffn_w_up, ffn_w_down):
    B, S, D = x.shape
    depth = mixer_norm_g.shape[0]
    nt = B * S // TS
    pos = jnp.broadcast_to(positions.reshape(nt, TS, 1), (nt, TS, HEAD_DIM))
    w_gate, w_up, w_down = ffn_w_gate.astype(bf16), ffn_w_up.astype(bf16), ffn_w_down.astype(bf16)
    w_in, w_pool, w_out = ret_w_in.astype(bf16), pool_w.astype(bf16), mix_w_out.astype(bf16)
    w_pw1, w_pw2 = conv_w_pw1.astype(bf16), conv_w_pw2.astype(bf16)
    h = x
    for layer in range(depth):
        i = layer // 2
        final = layer == depth - 1
        ffn = [_stacked(ffn_norm_g, layer), _stacked(w_gate, layer), _stacked(w_up, layer),
               _stacked(w_down, layer), _whole(final_norm_g[None, :])]
        if layer % 2 == 0:
            mix_params = [_stacked(mixer_norm_g, layer), _stacked(w_in, i), _stacked(ret_norm_g, i),
                          _stacked(w_pool, i), _stacked(pool_scale, i), _stacked(w_out, i)]
            h = _even_layer(h, pos, mix_params, ffn, final)
        else:
            mix_params = [_stacked(mixer_norm_g, layer), _stacked(w_pw1, i), _stacked(conv_b_pw1, i),
                          _stacked(conv_w_dw, i), _stacked(conv_b_dw, i), _stacked(conv_ln_g, i),
                          _stacked(conv_ln_b, i), _stacked(w_pw2, i), _stacked(conv_b_pw2, i)]
            h = _odd_layer(h, mix_params, ffn, final)
    return h
```

```python
import functools

import numpy as np
import jax
import jax.numpy as jnp
from jax import lax
from jax.experimental import pallas as pl
from jax.experimental.pallas import tpu as pltpu

D_MODEL = 1024
N_HEADS = 4
HEAD_DIM = 128
RET_WIDTH = N_HEADS * HEAD_DIM
POOL_WINDOWS = (2, 4, 8, 16)
POOL_GROUP = 128
POOL_WIDTH = len(POOL_WINDOWS) * POOL_GROUP
CHUNK = 128
ROPE_BASE = 10000.0
CONV_WIDTH = 31
EPS = 1e-6

LANES = 128
SUBLANES = 8
MXU_COLS = 256
VMEM_LIMIT_BYTES = 56 * 1024 * 1024

TS = 512

SEG = TS // SUBLANES
HALO = 32
Z_PITCH = 104
Y_PITCH = 72
N_SLABS = D_MODEL // LANES
CONV_PB = 4

POOL_HALO = 16

bf16 = jnp.bfloat16
f32 = jnp.float32


def _dot(a, b):
    return jnp.dot(a, b, preferred_element_type=f32)


def _swish(x):
    h = 0.5 * x
    return h + h * jnp.tanh(h)


def _glu(a, g):
    h = 0.5 * a
    return h + h * jnp.tanh(0.5 * g)


def _rms_norm(x, g):
    ms = jnp.mean(x * x, axis=-1, keepdims=True)
    return x * lax.rsqrt(ms + EPS) * g


def _ffn_stream(h1_ref, g_ref, wg_ref, wu_ref, wd_ref, fg_ref, hn_ref, act_ref, o_ref, final):
    x = h1_ref[...]
    hn_ref[...] = _rms_norm(x, g_ref[...]).astype(bf16)
    o_ref[0] = x
    yield
    d_ff = wg_ref.shape[1]
    for c in range(d_ff // MXU_COLS):
        cols = pl.ds(c * MXU_COLS, MXU_COLS)
        gate = _dot(hn_ref[...], wg_ref[:, cols])
        up = _dot(hn_ref[...], wu_ref[:, cols])
        act_ref[:, cols] = (_swish(gate) * up).astype(bf16)
        yield
    for c in range(D_MODEL // MXU_COLS):
        cols = pl.ds(c * MXU_COLS, MXU_COLS)
        o_ref[0, :, cols] = o_ref[0, :, cols] + _dot(act_ref[...], wd_ref[:, cols])
        yield
    if final:
        o_ref[0] = _rms_norm(o_ref[0], fg_ref[...])


def _even_init(first, state_ref, uext_ref):
    @pl.when(first)
    def _():
        state_ref[...] = jnp.zeros_like(state_ref)
        uext_ref[pl.ds(0, POOL_HALO), :] = jnp.zeros((POOL_HALO, POOL_WIDTH), f32)


def _even_stream(x_ref, h1_ref, tile_in_seq, pos_ref, ng_ref, win_ref, invf_ref, sign_ref, mask_ref,
                 qdec_ref, kdec_ref, rng_ref, pw_ref, ps_ref, wout_ref,
                 hn_ref, proj_ref, state_ref, uext_ref, mix_ref, chunk_decay):
    hn_ref[...] = _rms_norm(x_ref[0], ng_ref[...]).astype(bf16)
    in_width = win_ref.shape[1]
    for part in range(in_width // RET_WIDTH):
        cols = pl.ds(part * RET_WIDTH, RET_WIDTH)
        proj_ref[:, cols] = _dot(hn_ref[...], win_ref[:, cols])
        yield

    ang = pos_ref[0].astype(f32) * invf_ref[...]
    cos_t = jnp.cos(ang)
    sin_t = jnp.sin(ang) * sign_ref[...]

    def rope(t, cs, sn):
        return t * cs + pltpu.roll(t, HEAD_DIM // 2, axis=1) * sn

    heads = range(N_HEADS)
    for c in range(TS // CHUNK):
        rows = pl.ds(c * CHUNK, CHUNK)
        cs = cos_t[c * CHUNK:(c + 1) * CHUNK]
        sn = sin_t[c * CHUNK:(c + 1) * CHUNK]
        q = [rope(proj_ref[rows, pl.ds(h * HEAD_DIM, HEAD_DIM)], cs, sn) for h in heads]
        k = [rope(proj_ref[rows, pl.ds(RET_WIDTH + h * HEAD_DIM, HEAD_DIM)], cs, sn) for h in heads]
        v = [proj_ref[rows, pl.ds(2 * RET_WIDTH + h * HEAD_DIM, HEAD_DIM)].astype(bf16) for h in heads]
        scores = [lax.dot_general(q[h].astype(bf16), k[h].astype(bf16), (((1,), (1,)), ((), ())),
                                  preferred_element_type=f32) for h in heads]
        yield
        o = []
        for h in heads:
            sm = (scores[h] * mask_ref[h]).astype(bf16)
            qd = (q[h] * qdec_ref[h]).astype(bf16)
            st = state_ref[h]
            o.append(_dot(jnp.concatenate([sm, qd], axis=1),
                          jnp.concatenate([v[h], st.astype(bf16)], axis=0)))
            kd = (k[h] * kdec_ref[h]).astype(bf16)
            kv = lax.dot_general(kd, v[h], (((0,), (0,)), ((), ())), preferred_element_type=f32)
            state_ref[h] = st * chunk_decay[h] + kv
        yield
        for h in heads:
            col = pl.ds(h * HEAD_DIM, HEAD_DIM)
            gate = proj_ref[rows, pl.ds(3 * RET_WIDTH + h * HEAD_DIM, HEAD_DIM)]
            mu = jnp.mean(o[h], axis=-1, keepdims=True)
            oc = o[h] - mu
            var = jnp.mean(oc * oc, axis=-1, keepdims=True)
            on = oc * lax.rsqrt(var + EPS) * rng_ref[:, col]
            mix_ref[rows, col] = (_swish(gate) * on).astype(bf16)

    ucols = pl.ds(4 * RET_WIDTH, POOL_WIDTH)
    uext_ref[pl.ds(POOL_HALO, TS), :] = proj_ref[:, ucols]
    row = lax.broadcasted_iota(jnp.int32, (TS, POOL_GROUP), 0) + tile_in_seq * TS + 1
    for gi, w in enumerate(POOL_WINDOWS):
        gcols = pl.ds(gi * POOL_GROUP, POOL_GROUP)
        e = uext_ref[:, gcols]
        s = e
        step = 1
        while step < w:
            s = s + pltpu.roll(s, step, axis=0)
            step *= 2
        cnt = jnp.minimum(row, w).astype(f32)
        y = s[POOL_HALO:] / cnt - e[POOL_HALO:]
        pooled = _dot(y.astype(bf16), pw_ref[gi]) * ps_ref[:, gcols]
        mix_ref[:, pl.ds(RET_WIDTH + gi * POOL_GROUP, POOL_GROUP)] = pooled.astype(bf16)
    uext_ref[pl.ds(0, POOL_HALO), :] = uext_ref[pl.ds(TS, POOL_HALO), :]
    yield

    for c in range(D_MODEL // MXU_COLS):
        cols = pl.ds(c * MXU_COLS, MXU_COLS)
        h1_ref[:, cols] = x_ref[0, :, cols] + _dot(mix_ref[...], wout_ref[:, cols])
        yield


def _retention_constants():
    idx = np.arange(CHUNK, dtype=np.float32)
    log_gamma = np.log1p(-(np.float32(2.0) ** (-5.0 - np.arange(N_HEADS, dtype=np.float32)))).astype(np.float32)
    rel = idx[:, None] - idx[None, :]
    mask = np.where(rel >= 0, np.exp(log_gamma[:, None, None] * np.maximum(rel, 0.0)), 0.0).astype(np.float32)
    scale = np.float32(HEAD_DIM ** -0.5)
    q_decay = np.exp(log_gamma[:, None] * (idx + 1.0)).astype(np.float32)
    k_decay = np.exp(log_gamma[:, None] * (CHUNK - 1.0 - idx)).astype(np.float32)
    ones = np.ones((1, 1, HEAD_DIM), np.float32)
    return (mask * scale, q_decay[:, :, None] * ones, (k_decay * scale)[:, :, None] * ones)


def _chunk_decay():
    lg = np.log1p(-(2.0 ** (-5.0 - np.arange(N_HEADS, dtype=np.float32)))).astype(np.float32)
    return tuple(float(v) for v in np.exp(lg * np.float32(CHUNK)).astype(np.float32))


def _odd_init(first, carry_ref):
    @pl.when(first)
    def _():
        carry_ref[...] = jnp.zeros_like(carry_ref)


def _odd_glu_stream(x_ref, ng_ref, w1_ref, b1_ref, hn_ref, z_ref, carry_ref):
    hn_ref[...] = _rms_norm(x_ref[0], ng_ref[...]).astype(bf16)
    per_blk = MXU_COLS // LANES
    for c2 in range(D_MODEL // MXU_COLS):
        cols = pl.ds(c2 * MXU_COLS, MXU_COLS)
        gcols = pl.ds(D_MODEL + c2 * MXU_COLS, MXU_COLS)
        a = _dot(hn_ref[...], w1_ref[:, cols]) + b1_ref[:, cols]
        gt = _dot(hn_ref[...], w1_ref[:, gcols]) + b1_ref[:, gcols]
        yield
        u2 = _glu(a, gt)
        for ci in range(per_blk):
            c = c2 * per_blk + ci
            uc = u2[:, ci * LANES:(ci + 1) * LANES]
            z_ref[c, pl.ds(0, HALO), :] = carry_ref[c]
            for s in range(SUBLANES):
                z_ref[c, pl.ds(s * Z_PITCH + HALO, SEG), :] = uc[s * SEG:(s + 1) * SEG]
                tail = uc[(s + 1) * SEG - HALO:(s + 1) * SEG]
                if s + 1 < SUBLANES:
                    z_ref[c, pl.ds((s + 1) * Z_PITCH, HALO), :] = tail
                else:
                    carry_ref[c] = tail
    yield


def _odd_dwconv_stream(wdw_ref, bdw_ref, z_ref, e_ref, y_ref):
    base = HALO - (CONV_WIDTH - 1)
    n_e = SEG + CONV_WIDTH - 1
    per_blk = MXU_COLS // LANES
    for c in range(N_SLABS):
        for m in range(n_e):
            e_ref[c, pl.ds(m * SUBLANES, SUBLANES), :] = z_ref[c, pl.ds(base + m, SUBLANES, stride=Z_PITCH), :]
        lanes = pl.ds(c * LANES, LANES)
        wv = [jnp.broadcast_to(wdw_ref[pl.ds(k, 1), lanes], (SUBLANES, LANES)) for k in range(CONV_WIDTH)]
        bias = jnp.broadcast_to(bdw_ref[:, lanes], (SUBLANES, LANES))
        for p0 in range(0, SEG, CONV_PB):
            ev = [e_ref[c, pl.ds((p0 + m) * SUBLANES, SUBLANES), :] for m in range(CONV_PB + CONV_WIDTH - 1)]
            for i in range(CONV_PB):
                acc = bias
                for k in range(CONV_WIDTH):
                    acc = acc + ev[i + k] * wv[k]
                y_ref[c, pl.ds(p0 + i, SUBLANES, stride=Y_PITCH), :] = acc
        if c % per_blk == per_blk - 1:
            yield


def _odd_out_stream(res_ref, h1_ref, lng_ref, lnb_ref, w2_ref, b2_ref, hn_ref, y_ref):
    d = jnp.concatenate(
        [jnp.concatenate([y_ref[c, pl.ds(s * Y_PITCH, SEG), :] for c in range(N_SLABS)], axis=1)
         for s in range(SUBLANES)], axis=0)
    mu = jnp.mean(d, axis=-1, keepdims=True)
    dc = d - mu
    var = jnp.mean(dc * dc, axis=-1, keepdims=True)
    zz = dc * lax.rsqrt(var + EPS) * lng_ref[...] + lnb_ref[...]
    hn_ref[...] = _swish(zz).astype(bf16)
    yield
    for c in range(D_MODEL // MXU_COLS):
        cols = pl.ds(c * MXU_COLS, MXU_COLS)
        h1_ref[:, cols] = res_ref[:, cols] + _dot(hn_ref[...], w2_ref[:, cols]) + b2_ref[:, cols]
        yield


N_FFN_IN = 5


def _emit(order, streams):
    for key in order:
        next(streams[key])
    for gen in streams.values():
        for _ in gen:
            pass


def _split_refs(refs, n_mix_in, n_cast):
    refs = list(refs)
    take = lambda k: [refs.pop(0) for _ in range(k)]
    (x_ref,), mix_in, ffn_in, cast_in = take(1), take(n_mix_in), take(N_FFN_IN), take(n_cast)
    (o_ref,), cast_out = take(1), take(n_cast)
    return x_ref, mix_in, ffn_in, o_ref, refs[:3], refs[3:], list(zip(cast_in, cast_out))


def _cast_blocks(cast):
    for src, dst in cast:
        dst[...] = src[...].astype(bf16)


def _even_kernel(*refs, n_mix_in, n_cast, tiles_per_seq, final, chunk_decay):
    x_ref, mix_in, ffn_in, o_ref, (h1_ref, hnf_ref, act_ref), mix_scratch, cast = _split_refs(
        refs, n_mix_in, n_cast)
    n = pl.program_id(0)
    tile_in_seq = lax.rem(n, tiles_per_seq)

    @pl.when(n == 0)
    def _():
        h1_ref[...] = jnp.zeros((TS, D_MODEL), f32)

    _even_init(tile_in_seq == 0, mix_scratch[2], mix_scratch[3])
    ffn = _ffn_stream(h1_ref, *ffn_in, hnf_ref, act_ref, o_ref, final)
    mixer = _even_stream(x_ref, h1_ref, tile_in_seq, *mix_in, *mix_scratch, chunk_decay)
    next(ffn)
    _cast_blocks(cast)
    n_ffn = ffn_in[1].shape[1] // MXU_COLS + D_MODEL // MXU_COLS
    n_head = mix_in[2].shape[1] // RET_WIDTH + 2 * (TS // CHUNK) + 1
    n_tail = D_MODEL // MXU_COLS
    _emit("MF" * n_head + "F" * (n_ffn - n_head) + "M" * n_tail, {"F": ffn, "M": mixer})


def _odd_kernel(*refs, n_mix_in, n_cast, tiles_per_seq, final):
    x_ref, mix_in, ffn_in, o_ref, (h1_ref, hnf_ref, act_ref), mix_scratch, cast = _split_refs(
        refs, n_mix_in, n_cast)
    ng_ref, w1_ref, b1_ref, wdw_ref, bdw_ref, lng_ref, lnb_ref, w2_ref, b2_ref = mix_in
    hn_ref, z_ref, e_ref, y_ref, carry_ref, hn2_ref = mix_scratch
    n = pl.program_id(0)

    @pl.when(n == 0)
    def _():
        h1_ref[...] = jnp.zeros((TS, D_MODEL), f32)

    _odd_init(lax.rem(n, tiles_per_seq) == 0, carry_ref)
    ffn = _ffn_stream(h1_ref, *ffn_in, hnf_ref, act_ref, o_ref, final)
    glu = _odd_glu_stream(x_ref, ng_ref, w1_ref, b1_ref, hn_ref, z_ref, carry_ref)
    conv = _odd_dwconv_stream(wdw_ref, bdw_ref, z_ref, e_ref, y_ref)
    out = _odd_out_stream(x_ref.at[0], h1_ref, lng_ref, lnb_ref, w2_ref, b2_ref, hn2_ref, y_ref)
    next(ffn)
    _cast_blocks(cast)
    n_gate_up = ffn_in[1].shape[1] // MXU_COLS
    n_blk = D_MODEL // MXU_COLS
    order = "GFF" + "GFFF" * (n_blk - 1) + "G" + "F" * n_blk + "C" * n_blk + "O" * (n_blk + 1)
    assert (order.count("F"), order.count("G")) == (n_gate_up + n_blk, n_blk + 1)
    _emit(order, {"F": ffn, "G": glu, "C": conv, "O": out})


def _whole(arr):
    zeros = (0,) * arr.ndim
    return arr, pl.BlockSpec(arr.shape, lambda n: zeros, pipeline_mode=pl.Buffered(1))


def _stacked(arr, idx):
    if arr.ndim == 2:
        arr = arr[:, None, :]
    zeros = (0,) * (arr.ndim - 1)
    return arr, pl.BlockSpec((None,) + arr.shape[1:], lambda n: (idx,) + zeros, pipeline_mode=pl.Buffered(1))


def _cast_specs(arr, idx, n_steps):
    _, R, C = arr.shape
    rows = next(r for r in range(2 * SUBLANES, R + 1, 2 * SUBLANES) if R % r == 0 and R // r <= n_steps)
    last = R // rows - 1
    return (pl.BlockSpec((None, rows, C), lambda n: (idx, jnp.minimum(n, last), 0)),
            pl.BlockSpec((rows, C), lambda n: (jnp.minimum(n, last), 0)),
            jax.ShapeDtypeStruct((R, C), bf16))


def _layer(kind, h, mix, mix_scratch, ffn, cast, final):
    B, S, D = h.shape
    nt = B * S // TS
    d_ff = ffn[1][0].shape[-1]
    last = nt - 1
    tile_in = lambda n: (jnp.minimum(n, last), 0, 0)
    tile_out = lambda n: (jnp.maximum(n - 1, 0), 0, 0)
    static = dict(n_mix_in=len(mix), n_cast=len(cast), tiles_per_seq=S // TS, final=final)
    if kind == "even":
        body = functools.partial(_even_kernel, chunk_decay=_chunk_decay(), **static)
    else:
        body = functools.partial(_odd_kernel, **static)
    mix_specs = [pl.BlockSpec((1, TS, a.shape[-1]), tile_in) if s == "tile" else s for a, s in mix]
    cast_specs = [_cast_specs(a, idx, nt + 1) for a, idx in cast]
    outs = pl.pallas_call(
        body,
        grid=(nt + 1,),
        in_specs=[pl.BlockSpec((1, TS, D), tile_in)] + mix_specs + [s for _, s in ffn]
        + [s[0] for s in cast_specs],
        out_specs=[pl.BlockSpec((1, TS, D), tile_out)] + [s[1] for s in cast_specs],
        out_shape=[jax.ShapeDtypeStruct((nt, TS, D), h.dtype)] + [s[2] for s in cast_specs],
        scratch_shapes=[pltpu.VMEM((TS, D), f32), pltpu.VMEM((TS, D), bf16), pltpu.VMEM((TS, d_ff), bf16)]
        + mix_scratch,
        compiler_params=pltpu.CompilerParams(dimension_semantics=("arbitrary",),
                                             vmem_limit_bytes=VMEM_LIMIT_BYTES),
        name=kind + ("_layer_final" if final else "_layer"),
    )(h.reshape(nt, TS, D), *[a for a, _ in mix], *[a for a, _ in ffn], *[a for a, _ in cast])
    return outs[0].reshape(B, S, D), outs[1:]


def _even_layer(h, pos, mix_params, ffn, cast, final):
    B, S, D = h.shape
    in_width = mix_params[1][0].shape[-1]
    half = HEAD_DIM // 2
    inv_freq = ROPE_BASE ** (-jnp.arange(half, dtype=f32) / half)
    inv_freq = jnp.concatenate([inv_freq, inv_freq])[None, :]
    sign = jnp.concatenate([-jnp.ones((half,), f32), jnp.ones((half,), f32)])[None, :]
    norm_g, w_in, ret_norm_g, pool_w, pool_scale, w_out = mix_params
    mix = [(pos, "tile"), norm_g, w_in, _whole(inv_freq), _whole(sign)]
    mix += [_whole(jnp.asarray(c)) for c in _retention_constants()]
    mix += [ret_norm_g, pool_w, pool_scale, w_out]
    mix_scratch = [pltpu.VMEM((TS, D), bf16),
                   pltpu.VMEM((TS, in_width), f32),
                   pltpu.VMEM((N_HEADS, HEAD_DIM, HEAD_DIM), f32),
                   pltpu.VMEM((POOL_HALO + TS, POOL_WIDTH), f32),
                   pltpu.VMEM((TS, RET_WIDTH + POOL_WIDTH), bf16)]
    return _layer("even", h, mix, mix_scratch, ffn, cast, final)


def _odd_layer(h, mix_params, ffn, cast, final):
    B, S, D = h.shape
    mix_scratch = [pltpu.VMEM((TS, D), bf16),
                   pltpu.VMEM((N_SLABS, SUBLANES * Z_PITCH, LANES), f32),
                   pltpu.VMEM((N_SLABS, (SEG + CONV_WIDTH - 1) * SUBLANES, LANES), f32),
                   pltpu.VMEM((N_SLABS, SUBLANES * Y_PITCH, LANES), f32),
                   pltpu.VMEM((N_SLABS, HALO, LANES), f32),
                   pltpu.VMEM((TS, D), bf16)]
    return _layer("odd", h, list(mix_params), mix_scratch, ffn, cast, final)


def kernel(x, positions, mixer_norm_g, ffn_norm_g, final_norm_g, ret_w_in, ret_norm_g, pool_w, pool_scale, mix_w_out, conv_w_pw1, conv_b_pw1, conv_w_dw, conv_b_dw, conv_ln_g, conv_ln_b, conv_w_pw2, conv_b_pw2, ffn_w_gate, ffn_w_up, ffn_w_down):
    B, S, D = x.shape
    depth = mixer_norm_g.shape[0]
    nt = B * S // TS
    pos = jnp.broadcast_to(positions.reshape(nt, TS, 1), (nt, TS, HEAD_DIM))
    w_pool = pool_w.astype(bf16)

    def big_weights(layer):
        i = layer // 2
        mixer = [(ret_w_in, i), (mix_w_out, i)] if layer % 2 == 0 else [(conv_w_pw1, i), (conv_w_pw2, i)]
        return [(ffn_w_gate, layer), (ffn_w_up, layer), (ffn_w_down, layer)] + mixer

    weights = [a[idx].astype(bf16) for a, idx in big_weights(0)]
    h = x
    for layer in range(depth):
        i = layer // 2
        final = layer == depth - 1
        w_gate, w_up, w_down, w_mix_a, w_mix_b = [_whole(w) for w in weights]
        ffn = [_stacked(ffn_norm_g, layer), w_gate, w_up, w_down, _whole(final_norm_g[None, :])]
        cast = [] if final else big_weights(layer + 1)
        if layer % 2 == 0:
            mix_params = [_stacked(mixer_norm_g, layer), w_mix_a, _stacked(ret_norm_g, i),
                          _stacked(w_pool, i), _stacked(pool_scale, i), w_mix_b]
            h, weights = _even_layer(h, pos, mix_params, ffn, cast, final)
        else:
            mix_params = [_stacked(mixer_norm_g, layer), w_mix_a, _stacked(conv_b_pw1, i),
                          _stacked(conv_w_dw, i), _stacked(conv_b_dw, i), _stacked(conv_ln_g, i),
                          _stacked(conv_ln_b, i), w_mix_b, _stacked(conv_b_pw2, i)]
            h, weights = _odd_layer(h, mix_params, ffn, cast, final)
    return h
```

```python
import functools

import numpy as np
import jax
import jax.numpy as jnp
from jax import lax
from jax.experimental import pallas as pl
from jax.experimental.pallas import tpu as pltpu

D_MODEL = 1024
N_HEADS = 4
HEAD_DIM = 128
RET_WIDTH = N_HEADS * HEAD_DIM
POOL_WINDOWS = (2, 4, 8, 16)
POOL_GROUP = 128
POOL_WIDTH = len(POOL_WINDOWS) * POOL_GROUP
CHUNK = 128
ROPE_BASE = 10000.0
CONV_WIDTH = 31
EPS = 1e-6

LANES = 128
SUBLANES = 8
MXU_COLS = 256
VMEM_LIMIT_BYTES = 56 * 1024 * 1024

TS = 512

SEG = TS // SUBLANES
HALO = 32
Z_PITCH = 104
Y_PITCH = 72
N_SLABS = D_MODEL // LANES
CONV_PB = 4

POOL_HALO = 16

bf16 = jnp.bfloat16
f32 = jnp.float32


def _dot(a, b):
    return jnp.dot(a, b, preferred_element_type=f32)


def _swish(x):
    h = 0.5 * x
    return h + h * jnp.tanh(h)


def _glu(a, g):
    h = 0.5 * a
    return h + h * jnp.tanh(0.5 * g)


def _rms_norm(x, g):
    ms = jnp.mean(x * x, axis=-1, keepdims=True)
    return x * lax.rsqrt(ms + EPS) * g


def _ffn_stream(h1_ref, g_ref, wg_ref, wu_ref, wd_ref, fg_ref, hn_ref, act_ref, o_ref, final):
    x = h1_ref[...]
    hn_ref[...] = _rms_norm(x, g_ref[...]).astype(bf16)
    o_ref[0] = x
    yield
    d_ff = wg_ref.shape[1]
    for c in range(d_ff // MXU_COLS):
        cols = pl.ds(c * MXU_COLS, MXU_COLS)
        gate = _dot(hn_ref[...], wg_ref[:, cols])
        up = _dot(hn_ref[...], wu_ref[:, cols])
        act_ref[:, cols] = (_swish(gate) * up).astype(bf16)
        yield
    for c in range(D_MODEL // MXU_COLS):
        cols = pl.ds(c * MXU_COLS, MXU_COLS)
        o_ref[0, :, cols] = o_ref[0, :, cols] + _dot(act_ref[...], wd_ref[:, cols])
        yield
    if final:
        o_ref[0] = _rms_norm(o_ref[0], fg_ref[...])


def _even_init(first, state_ref, uext_ref):
    @pl.when(first)
    def _():
        state_ref[...] = jnp.zeros_like(state_ref)
        uext_ref[pl.ds(0, POOL_HALO), :] = jnp.zeros((POOL_HALO, POOL_WIDTH), f32)


def _even_stream(x_ref, h1_ref, tile_in_seq, pos_ref, ng_ref, win_ref, invf_ref, sign_ref, mask_ref,
                 qdec_ref, kdec_ref, rng_ref, pw_ref, ps_ref, wout_ref,
                 hn_ref, proj_ref, state_ref, uext_ref, mix_ref, chunk_decay):
    hn_ref[...] = _rms_norm(x_ref[0], ng_ref[...]).astype(bf16)
    in_width = win_ref.shape[1]
    for part in range(in_width // RET_WIDTH):
        cols = pl.ds(part * RET_WIDTH, RET_WIDTH)
        proj_ref[:, cols] = _dot(hn_ref[...], win_ref[:, cols])
        yield

    ang = pos_ref[0].astype(f32) * invf_ref[...]
    cos_t = jnp.cos(ang)
    sin_t = jnp.sin(ang) * sign_ref[...]

    def rope(t, cs, sn):
        return t * cs + pltpu.roll(t, HEAD_DIM // 2, axis=1) * sn

    heads = range(N_HEADS)
    for c in range(TS // CHUNK):
        rows = pl.ds(c * CHUNK, CHUNK)
        cs = cos_t[c * CHUNK:(c + 1) * CHUNK]
        sn = sin_t[c * CHUNK:(c + 1) * CHUNK]
        q = [rope(proj_ref[rows, pl.ds(h * HEAD_DIM, HEAD_DIM)], cs, sn) for h in heads]
        k = [rope(proj_ref[rows, pl.ds(RET_WIDTH + h * HEAD_DIM, HEAD_DIM)], cs, sn) for h in heads]
        v = [proj_ref[rows, pl.ds(2 * RET_WIDTH + h * HEAD_DIM, HEAD_DIM)].astype(bf16) for h in heads]
        scores = [lax.dot_general(q[h].astype(bf16), k[h].astype(bf16), (((1,), (1,)), ((), ())),
                                  preferred_element_type=f32) for h in heads]
        yield
        o = []
        for h in heads:
            sm = (scores[h] * mask_ref[h]).astype(bf16)
            qd = (q[h] * qdec_ref[h]).astype(bf16)
            st = state_ref[h]
            o.append(_dot(jnp.concatenate([sm, qd], axis=1),
                          jnp.concatenate([v[h], st.astype(bf16)], axis=0)))
            kd = (k[h] * kdec_ref[h]).astype(bf16)
            kv = lax.dot_general(kd, v[h], (((0,), (0,)), ((), ())), preferred_element_type=f32)
            state_ref[h] = st * chunk_decay[h] + kv
        yield
        for h in heads:
            col = pl.ds(h * HEAD_DIM, HEAD_DIM)
            gate = proj_ref[rows, pl.ds(3 * RET_WIDTH + h * HEAD_DIM, HEAD_DIM)]
            mu = jnp.mean(o[h], axis=-1, keepdims=True)
            oc = o[h] - mu
            var = jnp.mean(oc * oc, axis=-1, keepdims=True)
            on = oc * lax.rsqrt(var + EPS) * rng_ref[:, col]
            mix_ref[rows, col] = (_swish(gate) * on).astype(bf16)

    ucols = pl.ds(4 * RET_WIDTH, POOL_WIDTH)
    uext_ref[pl.ds(POOL_HALO, TS), :] = proj_ref[:, ucols]
    row = lax.broadcasted_iota(jnp.int32, (TS, POOL_GROUP), 0) + tile_in_seq * TS + 1
    for gi, w in enumerate(POOL_WINDOWS):
        gcols = pl.ds(gi * POOL_GROUP, POOL_GROUP)
        e = uext_ref[:, gcols]
        s = e
        step = 1
        while step < w:
            s = s + pltpu.roll(s, step, axis=0)
            step *= 2
        cnt = jnp.minimum(row, w).astype(f32)
        y = s[POOL_HALO:] / cnt - e[POOL_HALO:]
        pooled = _dot(y.astype(bf16), pw_ref[gi]) * ps_ref[:, gcols]
        mix_ref[:, pl.ds(RET_WIDTH + gi * POOL_GROUP, POOL_GROUP)] = pooled.astype(bf16)
    uext_ref[pl.ds(0, POOL_HALO), :] = uext_ref[pl.ds(TS, POOL_HALO), :]
    yield

    for c in range(D_MODEL // MXU_COLS):
        cols = pl.ds(c * MXU_COLS, MXU_COLS)
        h1_ref[:, cols] = x_ref[0, :, cols] + _dot(mix_ref[...], wout_ref[:, cols])
        yield


def _retention_constants():
    idx = np.arange(CHUNK, dtype=np.float32)
    log_gamma = np.log1p(-(np.float32(2.0) ** (-5.0 - np.arange(N_HEADS, dtype=np.float32)))).astype(np.float32)
    rel = idx[:, None] - idx[None, :]
    mask = np.where(rel >= 0, np.exp(log_gamma[:, None, None] * np.maximum(rel, 0.0)), 0.0).astype(np.float32)
    scale = np.float32(HEAD_DIM ** -0.5)
    q_decay = np.exp(log_gamma[:, None] * (idx + 1.0)).astype(np.float32)
    k_decay = np.exp(log_gamma[:, None] * (CHUNK - 1.0 - idx)).astype(np.float32)
    ones = np.ones((1, 1, HEAD_DIM), np.float32)
    return (mask * scale, q_decay[:, :, None] * ones, (k_decay * scale)[:, :, None] * ones)


def _chunk_decay():
    lg = np.log1p(-(2.0 ** (-5.0 - np.arange(N_HEADS, dtype=np.float32)))).astype(np.float32)
    return tuple(float(v) for v in np.exp(lg * np.float32(CHUNK)).astype(np.float32))


def _odd_init(first, carry_ref):
    @pl.when(first)
    def _():
        carry_ref[...] = jnp.zeros_like(carry_ref)


def _odd_glu_stream(x_ref, ng_ref, w1_ref, b1_ref, hn_ref, z_ref, carry_ref):
    hn_ref[...] = _rms_norm(x_ref[0], ng_ref[...]).astype(bf16)
    per_blk = MXU_COLS // LANES
    for c2 in range(D_MODEL // MXU_COLS):
        cols = pl.ds(c2 * MXU_COLS, MXU_COLS)
        gcols = pl.ds(D_MODEL + c2 * MXU_COLS, MXU_COLS)
        a = _dot(hn_ref[...], w1_ref[:, cols]) + b1_ref[:, cols]
        gt = _dot(hn_ref[...], w1_ref[:, gcols]) + b1_ref[:, gcols]
        yield
        u2 = _glu(a, gt)
        for ci in range(per_blk):
            c = c2 * per_blk + ci
            uc = u2[:, ci * LANES:(ci + 1) * LANES]
            z_ref[c, pl.ds(0, HALO), :] = carry_ref[c]
            for s in range(SUBLANES):
                z_ref[c, pl.ds(s * Z_PITCH + HALO, SEG), :] = uc[s * SEG:(s + 1) * SEG]
                tail = uc[(s + 1) * SEG - HALO:(s + 1) * SEG]
                if s + 1 < SUBLANES:
                    z_ref[c, pl.ds((s + 1) * Z_PITCH, HALO), :] = tail
                else:
                    carry_ref[c] = tail
    yield


def _odd_dwconv_stream(wdw_ref, bdw_ref, z_ref, e_ref, y_ref):
    base = HALO - (CONV_WIDTH - 1)
    n_e = SEG + CONV_WIDTH - 1
    per_blk = MXU_COLS // LANES
    for c in range(N_SLABS):
        for m in range(n_e):
            e_ref[c, pl.ds(m * SUBLANES, SUBLANES), :] = z_ref[c, pl.ds(base + m, SUBLANES, stride=Z_PITCH), :]
        lanes = pl.ds(c * LANES, LANES)
        wv = [jnp.broadcast_to(wdw_ref[pl.ds(k, 1), lanes], (SUBLANES, LANES)) for k in range(CONV_WIDTH)]
        bias = jnp.broadcast_to(bdw_ref[:, lanes], (SUBLANES, LANES))
        for p0 in range(0, SEG, CONV_PB):
            ev = [e_ref[c, pl.ds((p0 + m) * SUBLANES, SUBLANES), :] for m in range(CONV_PB + CONV_WIDTH - 1)]
            for i in range(CONV_PB):
                acc = bias
                for k in range(CONV_WIDTH):
                    acc = acc + ev[i + k] * wv[k]
                y_ref[c, pl.ds(p0 + i, SUBLANES, stride=Y_PITCH), :] = acc
        if c % per_blk == per_blk - 1:
            yield


def _odd_out_stream(res_ref, h1_ref, lng_ref, lnb_ref, w2_ref, b2_ref, hn_ref, y_ref):
    d = jnp.concatenate(
        [jnp.concatenate([y_ref[c, pl.ds(s * Y_PITCH, SEG), :] for c in range(N_SLABS)], axis=1)
         for s in range(SUBLANES)], axis=0)
    mu = jnp.mean(d, axis=-1, keepdims=True)
    dc = d - mu
    var = jnp.mean(dc * dc, axis=-1, keepdims=True)
    zz = dc * lax.rsqrt(var + EPS) * lng_ref[...] + lnb_ref[...]
    hn_ref[...] = _swish(zz).astype(bf16)
    yield
    for c in range(D_MODEL // MXU_COLS):
        cols = pl.ds(c * MXU_COLS, MXU_COLS)
        h1_ref[:, cols] = res_ref[:, cols] + _dot(hn_ref[...], w2_ref[:, cols]) + b2_ref[:, cols]
        yield


N_FFN_IN = 5


def _emit(order, streams):
    for key in order:
        next(streams[key])
    for gen in streams.values():
        for _ in gen:
            pass


def _split_refs(refs, n_mix_in, n_cast):
    refs = list(refs)
    take = lambda k: [refs.pop(0) for _ in range(k)]
    (x_ref,), mix_in, ffn_in, cast_in = take(1), take(n_mix_in), take(N_FFN_IN), take(n_cast)
    (o_ref,), cast_out = take(1), take(n_cast)
    return x_ref, mix_in, ffn_in, o_ref, refs[:3], refs[3:], list(zip(cast_in, cast_out))


def _cast_blocks(cast):
    for src, dst in cast:
        dst[...] = src[...].astype(bf16)


def _even_kernel(*refs, n_mix_in, n_cast, tiles_per_seq, final, chunk_decay):
    x_ref, mix_in, ffn_in, o_ref, (h1_ref, hnf_ref, act_ref), mix_scratch, cast = _split_refs(
        refs, n_mix_in, n_cast)
    n = pl.program_id(0)
    last = pl.num_programs(0) - 1
    tile_in_seq = lax.rem(n, tiles_per_seq)
    _even_init(tile_in_seq == 0, mix_scratch[2], mix_scratch[3])

    def ffn_stream():
        return _ffn_stream(h1_ref, *ffn_in, hnf_ref, act_ref, o_ref, final)

    def mixer_stream():
        return _even_stream(x_ref, h1_ref, tile_in_seq, *mix_in, *mix_scratch, chunk_decay)

    @pl.when(n == 0)
    def _():
        _cast_blocks(cast)
        _emit("", {"M": mixer_stream()})

    @pl.when(n == last)
    def _():
        _cast_blocks(cast)
        _emit("", {"F": ffn_stream()})

    @pl.when(jnp.logical_and(n > 0, n < last))
    def _():
        ffn, mixer = ffn_stream(), mixer_stream()
        next(ffn)
        _cast_blocks(cast)
        n_ffn = ffn_in[1].shape[1] // MXU_COLS + D_MODEL // MXU_COLS
        n_head = mix_in[2].shape[1] // RET_WIDTH + 2 * (TS // CHUNK) + 1
        n_tail = D_MODEL // MXU_COLS
        _emit("MF" * n_head + "F" * (n_ffn - n_head) + "M" * n_tail, {"F": ffn, "M": mixer})


def _odd_kernel(*refs, n_mix_in, n_cast, tiles_per_seq, final):
    x_ref, mix_in, ffn_in, o_ref, (h1_ref, hnf_ref, act_ref), mix_scratch, cast = _split_refs(
        refs, n_mix_in, n_cast)
    ng_ref, w1_ref, b1_ref, wdw_ref, bdw_ref, lng_ref, lnb_ref, w2_ref, b2_ref = mix_in
    hn_ref, z_ref, e_ref, y_ref, carry_ref, hn2_ref = mix_scratch
    n = pl.program_id(0)
    last = pl.num_programs(0) - 1
    _odd_init(lax.rem(n, tiles_per_seq) == 0, carry_ref)

    def ffn_stream():
        return _ffn_stream(h1_ref, *ffn_in, hnf_ref, act_ref, o_ref, final)

    def mixer_streams():
        return {"G": _odd_glu_stream(x_ref, ng_ref, w1_ref, b1_ref, hn_ref, z_ref, carry_ref),
                "C": _odd_dwconv_stream(wdw_ref, bdw_ref, z_ref, e_ref, y_ref),
                "O": _odd_out_stream(x_ref.at[0], h1_ref, lng_ref, lnb_ref, w2_ref, b2_ref, hn2_ref, y_ref)}

    @pl.when(n == 0)
    def _():
        _cast_blocks(cast)
        _emit("", mixer_streams())

    @pl.when(n == last)
    def _():
        _cast_blocks(cast)
        _emit("", {"F": ffn_stream()})

    @pl.when(jnp.logical_and(n > 0, n < last))
    def _():
        streams = dict(F=ffn_stream(), **mixer_streams())
        next(streams["F"])
        _cast_blocks(cast)
        n_gate_up = ffn_in[1].shape[1] // MXU_COLS
        n_blk = D_MODEL // MXU_COLS
        order = "GFF" + "GFFF" * (n_blk - 1) + "G" + "F" * n_blk + "C" * n_blk + "O" * (n_blk + 1)
        assert (order.count("F"), order.count("G")) == (n_gate_up + n_blk, n_blk + 1)
        _emit(order, streams)


def _whole(arr):
    zeros = (0,) * arr.ndim
    return arr, pl.BlockSpec(arr.shape, lambda n: zeros, pipeline_mode=pl.Buffered(1))


def _stacked(arr, idx):
    if arr.ndim == 2:
        arr = arr[:, None, :]
    zeros = (0,) * (arr.ndim - 1)
    return arr, pl.BlockSpec((None,) + arr.shape[1:], lambda n: (idx,) + zeros, pipeline_mode=pl.Buffered(1))


def _cast_specs(arr, idx, n_steps):
    _, R, C = arr.shape
    rows = next(r for r in range(2 * SUBLANES, R + 1, 2 * SUBLANES) if R % r == 0 and R // r <= n_steps)
    last = R // rows - 1
    return (pl.BlockSpec((None, rows, C), lambda n: (idx, jnp.minimum(n, last), 0)),
            pl.BlockSpec((rows, C), lambda n: (jnp.minimum(n, last), 0)),
            jax.ShapeDtypeStruct((R, C), bf16))


def _layer(kind, h, mix, mix_scratch, ffn, cast, final):
    B, S, D = h.shape
    nt = B * S // TS
    d_ff = ffn[1][0].shape[-1]
    last = nt - 1
    tile_in = lambda n: (jnp.minimum(n, last), 0, 0)
    tile_out = lambda n: (jnp.maximum(n - 1, 0), 0, 0)
    static = dict(n_mix_in=len(mix), n_cast=len(cast), tiles_per_seq=S // TS, final=final)
    if kind == "even":
        body = functools.partial(_even_kernel, chunk_decay=_chunk_decay(), **static)
    else:
        body = functools.partial(_odd_kernel, **static)
    mix_specs = [pl.BlockSpec((1, TS, a.shape[-1]), tile_in) if s == "tile" else s for a, s in mix]
    cast_specs = [_cast_specs(a, idx, nt + 1) for a, idx in cast]
    outs = pl.pallas_call(
        body,
        grid=(nt + 1,),
        in_specs=[pl.BlockSpec((1, TS, D), tile_in)] + mix_specs + [s for _, s in ffn]
        + [s[0] for s in cast_specs],
        out_specs=[pl.BlockSpec((1, TS, D), tile_out)] + [s[1] for s in cast_specs],
        out_shape=[jax.ShapeDtypeStruct((nt, TS, D), h.dtype)] + [s[2] for s in cast_specs],
        scratch_shapes=[pltpu.VMEM((TS, D), f32), pltpu.VMEM((TS, D), bf16), pltpu.VMEM((TS, d_ff), bf16)]
        + mix_scratch,
        compiler_params=pltpu.CompilerParams(dimension_semantics=("arbitrary",),
                                             vmem_limit_bytes=VMEM_LIMIT_BYTES),
        name=kind + ("_layer_final" if final else "_layer"),
    )(h.reshape(nt, TS, D), *[a for a, _ in mix], *[a for a, _ in ffn], *[a for a, _ in cast])
    return outs[0].reshape(B, S, D), outs[1:]


def _even_layer(h, pos, mix_params, ffn, cast, final):
    B, S, D = h.shape
    in_width = mix_params[1][0].shape[-1]
    half = HEAD_DIM // 2
    inv_freq = ROPE_BASE ** (-jnp.arange(half, dtype=f32) / half)
    inv_freq = jnp.concatenate([inv_freq, inv_freq])[None, :]
    sign = jnp.concatenate([-jnp.ones((half,), f32), jnp.ones((half,), f32)])[None, :]
    norm_g, w_in, ret_norm_g, pool_w, pool_scale, w_out = mix_params
    mix = [(pos, "tile"), norm_g, w_in, _whole(inv_freq), _whole(sign)]
    mix += [_whole(jnp.asarray(c)) for c in _retention_constants()]
    mix += [ret_norm_g, pool_w, pool_scale, w_out]
    mix_scratch = [pltpu.VMEM((TS, D), bf16),
                   pltpu.VMEM((TS, in_width), f32),
                   pltpu.VMEM((N_HEADS, HEAD_DIM, HEAD_DIM), f32),
                   pltpu.VMEM((POOL_HALO + TS, POOL_WIDTH), f32),
                   pltpu.VMEM((TS, RET_WIDTH + POOL_WIDTH), bf16)]
    return _layer("even", h, mix, mix_scratch, ffn, cast, final)


def _odd_layer(h, mix_params, ffn, cast, final):
    B, S, D = h.shape
    mix_scratch = [pltpu.VMEM((TS, D), bf16),
                   pltpu.VMEM((N_SLABS, SUBLANES * Z_PITCH, LANES), f32),
                   pltpu.VMEM((N_SLABS, (SEG + CONV_WIDTH - 1) * SUBLANES, LANES), f32),
                   pltpu.VMEM((N_SLABS, SUBLANES * Y_PITCH, LANES), f32),
                   pltpu.VMEM((N_SLABS, HALO, LANES), f32),
                   pltpu.VMEM((TS, D), bf16)]
    return _layer("odd", h, list(mix_params), mix_scratch, ffn, cast, final)


def kernel(x, positions, mixer_norm_g, ffn_norm_g, final_norm_g, ret_w_in, ret_norm_g, pool_w, pool_scale, mix_w_out, conv_w_pw1, conv_b_pw1, conv_w_dw, conv_b_dw, conv_ln_g, conv_ln_b, conv_w_pw2, conv_b_pw2, ffn_w_gate, ffn_w_up, ffn_w_down):
    B, S, D = x.shape
    depth = mixer_norm_g.shape[0]
    nt = B * S // TS
    pos = jnp.broadcast_to(positions.reshape(nt, TS, 1), (nt, TS, HEAD_DIM))
    w_pool = pool_w.astype(bf16)

    def big_weights(layer):
        i = layer // 2
        mixer = [(ret_w_in, i), (mix_w_out, i)] if layer % 2 == 0 else [(conv_w_pw1, i), (conv_w_pw2, i)]
        return [(ffn_w_gate, layer), (ffn_w_up, layer), (ffn_w_down, layer)] + mixer

    weights = [a[idx].astype(bf16) for a, idx in big_weights(0)]
    h = x
    for layer in range(depth):
        i = layer // 2
        final = layer == depth - 1
        w_gate, w_up, w_down, w_mix_a, w_mix_b = [_whole(w) for w in weights]
        ffn = [_stacked(ffn_norm_g, layer), w_gate, w_up, w_down, _whole(final_norm_g[None, :])]
        cast = [] if final else big_weights(layer + 1)
        if layer % 2 == 0:
            mix_params = [_stacked(mixer_norm_g, layer), w_mix_a, _stacked(ret_norm_g, i),
                          _stacked(w_pool, i), _stacked(pool_scale, i), w_mix_b]
            h, weights = _even_layer(h, pos, mix_params, ffn, cast, final)
        else:
            mix_params = [_stacked(mixer_norm_g, layer), w_mix_a, _stacked(conv_b_pw1, i),
                          _stacked(conv_w_dw, i), _stacked(conv_b_dw, i), _stacked(conv_ln_g, i),
                          _stacked(conv_ln_b, i), w_mix_b, _stacked(conv_b_pw2, i)]
            h, weights = _odd_layer(h, mix_params, ffn, cast, final)
    return h
```

```python
import functools

import numpy as np
import jax
import jax.numpy as jnp
from jax import lax
from jax.experimental import pallas as pl
from jax.experimental.pallas import tpu as pltpu

D_MODEL = 1024
N_HEADS = 4
HEAD_DIM = 128
RET_WIDTH = N_HEADS * HEAD_DIM
POOL_WINDOWS = (2, 4, 8, 16)
POOL_GROUP = 128
POOL_WIDTH = len(POOL_WINDOWS) * POOL_GROUP
CHUNK = 128
ROPE_BASE = 10000.0
CONV_WIDTH = 31
EPS = 1e-6

LANES = 128
SUBLANES = 8
MXU_COLS = 256
VMEM_LIMIT_BYTES = 56 * 1024 * 1024

TS = 512

SEG = TS // SUBLANES
HALO = 32
Z_PITCH = 104
Y_PITCH = 72
N_SLABS = D_MODEL // LANES
CONV_PB = 4

POOL_HALO = 16

bf16 = jnp.bfloat16
f32 = jnp.float32


def _dot(a, b):
    return jnp.dot(a, b, preferred_element_type=f32)


def _swish(x):
    h = 0.5 * x
    return h + h * jnp.tanh(h)


def _glu(a, g):
    h = 0.5 * a
    return h + h * jnp.tanh(0.5 * g)


def _rms_norm(x, g):
    ms = jnp.mean(x * x, axis=-1, keepdims=True)
    return x * lax.rsqrt(ms + EPS) * g


def _ffn_stream(h1_ref, g_ref, wg_ref, wu_ref, wd_ref, fg_ref, hn_ref, act_ref, o_ref, final):
    x = h1_ref[...]
    hn_ref[...] = _rms_norm(x, g_ref[...]).astype(bf16)
    o_ref[0] = x
    yield
    d_ff = wg_ref.shape[1]
    for c in range(d_ff // MXU_COLS):
        cols = pl.ds(c * MXU_COLS, MXU_COLS)
        gate = _dot(hn_ref[...], wg_ref[:, cols])
        up = _dot(hn_ref[...], wu_ref[:, cols])
        act_ref[:, cols] = (_swish(gate) * up).astype(bf16)
        yield
    for c in range(D_MODEL // MXU_COLS):
        cols = pl.ds(c * MXU_COLS, MXU_COLS)
        o_ref[0, :, cols] = o_ref[0, :, cols] + _dot(act_ref[...], wd_ref[:, cols])
        yield
    if final:
        o_ref[0] = _rms_norm(o_ref[0], fg_ref[...])


def _even_init(first, state_ref, uext_ref):
    @pl.when(first)
    def _():
        state_ref[...] = jnp.zeros_like(state_ref)
        uext_ref[pl.ds(0, POOL_HALO), :] = jnp.zeros((POOL_HALO, POOL_WIDTH), f32)


def _even_stream(x_ref, h1_ref, tile_in_seq, pos_ref, ng_ref, win_ref, invf_ref, sign_ref, mask_ref,
                 qdec_ref, kdec_ref, rng_ref, pw_ref, ps_ref, wout_ref,
                 hn_ref, proj_ref, state_ref, uext_ref, mix_ref, chunk_decay):
    hn_ref[...] = _rms_norm(x_ref[0], ng_ref[...]).astype(bf16)
    in_width = win_ref.shape[1]
    for part in range(in_width // RET_WIDTH):
        cols = pl.ds(part * RET_WIDTH, RET_WIDTH)
        proj_ref[:, cols] = _dot(hn_ref[...], win_ref[:, cols])
        yield

    ang = pos_ref[0].astype(f32) * invf_ref[...]
    cos_t = jnp.cos(ang)
    sin_t = jnp.sin(ang) * sign_ref[...]

    def rope(t, cs, sn):
        return t * cs + pltpu.roll(t, HEAD_DIM // 2, axis=1) * sn

    heads = range(N_HEADS)
    for c in range(TS // CHUNK):
        rows = pl.ds(c * CHUNK, CHUNK)
        cs = cos_t[c * CHUNK:(c + 1) * CHUNK]
        sn = sin_t[c * CHUNK:(c + 1) * CHUNK]
        q = [rope(proj_ref[rows, pl.ds(h * HEAD_DIM, HEAD_DIM)], cs, sn) for h in heads]
        k = [rope(proj_ref[rows, pl.ds(RET_WIDTH + h * HEAD_DIM, HEAD_DIM)], cs, sn) for h in heads]
        v = [proj_ref[rows, pl.ds(2 * RET_WIDTH + h * HEAD_DIM, HEAD_DIM)].astype(bf16) for h in heads]
        scores = [lax.dot_general(q[h].astype(bf16), k[h].astype(bf16), (((1,), (1,)), ((), ())),
                                  preferred_element_type=f32) for h in heads]
        yield
        o = []
        for h in heads:
            sm = (scores[h] * mask_ref[h]).astype(bf16)
            qd = (q[h] * qdec_ref[h]).astype(bf16)
            st = state_ref[h]
            o.append(_dot(jnp.concatenate([sm, qd], axis=1),
                          jnp.concatenate([v[h], st.astype(bf16)], axis=0)))
            kd = (k[h] * kdec_ref[h]).astype(bf16)
            kv = lax.dot_general(kd, v[h], (((0,), (0,)), ((), ())), preferred_element_type=f32)
            state_ref[h] = st * chunk_decay[h] + kv
        yield
        for h in heads:
            col = pl.ds(h * HEAD_DIM, HEAD_DIM)
            gate = proj_ref[rows, pl.ds(3 * RET_WIDTH + h * HEAD_DIM, HEAD_DIM)]
            mu = jnp.mean(o[h], axis=-1, keepdims=True)
            oc = o[h] - mu
            var = jnp.mean(oc * oc, axis=-1, keepdims=True)
            on = oc * lax.rsqrt(var + EPS) * rng_ref[:, col]
            mix_ref[rows, col] = (_swish(gate) * on).astype(bf16)

    ucols = pl.ds(4 * RET_WIDTH, POOL_WIDTH)
    uext_ref[pl.ds(POOL_HALO, TS), :] = proj_ref[:, ucols]
    row = lax.broadcasted_iota(jnp.int32, (TS, POOL_GROUP), 0) + tile_in_seq * TS + 1
    for gi, w in enumerate(POOL_WINDOWS):
        gcols = pl.ds(gi * POOL_GROUP, POOL_GROUP)
        e = uext_ref[:, gcols]
        s = e
        step = 1
        while step < w:
            s = s + pltpu.roll(s, step, axis=0)
            step *= 2
        cnt = jnp.minimum(row, w).astype(f32)
        y = s[POOL_HALO:] / cnt - e[POOL_HALO:]
        pooled = _dot(y.astype(bf16), pw_ref[gi]) * ps_ref[:, gcols]
        mix_ref[:, pl.ds(RET_WIDTH + gi * POOL_GROUP, POOL_GROUP)] = pooled.astype(bf16)
    uext_ref[pl.ds(0, POOL_HALO), :] = uext_ref[pl.ds(TS, POOL_HALO), :]
    yield

    for c in range(D_MODEL // MXU_COLS):
        cols = pl.ds(c * MXU_COLS, MXU_COLS)
        h1_ref[:, cols] = x_ref[0, :, cols] + _dot(mix_ref[...], wout_ref[:, cols])
        yield


def _retention_constants():
    idx = np.arange(CHUNK, dtype=np.float32)
    log_gamma = np.log1p(-(np.float32(2.0) ** (-5.0 - np.arange(N_HEADS, dtype=np.float32)))).astype(np.float32)
    rel = idx[:, None] - idx[None, :]
    mask = np.where(rel >= 0, np.exp(log_gamma[:, None, None] * np.maximum(rel, 0.0)), 0.0).astype(np.float32)
    scale = np.float32(HEAD_DIM ** -0.5)
    q_decay = np.exp(log_gamma[:, None] * (idx + 1.0)).astype(np.float32)
    k_decay = np.exp(log_gamma[:, None] * (CHUNK - 1.0 - idx)).astype(np.float32)
    ones = np.ones((1, 1, HEAD_DIM), np.float32)
    return (mask * scale, q_decay[:, :, None] * ones, (k_decay * scale)[:, :, None] * ones)


def _chunk_decay():
    lg = np.log1p(-(2.0 ** (-5.0 - np.arange(N_HEADS, dtype=np.float32)))).astype(np.float32)
    return tuple(float(v) for v in np.exp(lg * np.float32(CHUNK)).astype(np.float32))


def _odd_init(first, carry_ref):
    @pl.when(first)
    def _():
        carry_ref[...] = jnp.zeros_like(carry_ref)


def _odd_glu_stream(x_ref, ng_ref, w1_ref, b1_ref, hn_ref, z_ref, carry_ref):
    hn_ref[...] = _rms_norm(x_ref[0], ng_ref[...]).astype(bf16)
    per_blk = MXU_COLS // LANES
    for c2 in range(D_MODEL // MXU_COLS):
        cols = pl.ds(c2 * MXU_COLS, MXU_COLS)
        gcols = pl.ds(D_MODEL + c2 * MXU_COLS, MXU_COLS)
        a = _dot(hn_ref[...], w1_ref[:, cols]) + b1_ref[:, cols]
        gt = _dot(hn_ref[...], w1_ref[:, gcols]) + b1_ref[:, gcols]
        yield
        u2 = _glu(a, gt)
        for ci in range(per_blk):
            c = c2 * per_blk + ci
            uc = u2[:, ci * LANES:(ci + 1) * LANES]
            z_ref[c, pl.ds(0, HALO), :] = carry_ref[c]
            for s in range(SUBLANES):
                z_ref[c, pl.ds(s * Z_PITCH + HALO, SEG), :] = uc[s * SEG:(s + 1) * SEG]
                tail = uc[(s + 1) * SEG - HALO:(s + 1) * SEG]
                if s + 1 < SUBLANES:
                    z_ref[c, pl.ds((s + 1) * Z_PITCH, HALO), :] = tail
                else:
                    carry_ref[c] = tail
    yield


def _odd_dwconv_stream(wdw_ref, bdw_ref, z_ref, e_ref, y_ref):
    base = HALO - (CONV_WIDTH - 1)
    n_e = SEG + CONV_WIDTH - 1
    per_blk = MXU_COLS // LANES
    for c in range(N_SLABS):
        for m in range(n_e):
            e_ref[c, pl.ds(m * SUBLANES, SUBLANES), :] = z_ref[c, pl.ds(base + m, SUBLANES, stride=Z_PITCH), :]
        lanes = pl.ds(c * LANES, LANES)
        wv = [jnp.broadcast_to(wdw_ref[pl.ds(k, 1), lanes], (SUBLANES, LANES)) for k in range(CONV_WIDTH)]
        bias = jnp.broadcast_to(bdw_ref[:, lanes], (SUBLANES, LANES))
        for p0 in range(0, SEG, CONV_PB):
            ev = [e_ref[c, pl.ds((p0 + m) * SUBLANES, SUBLANES), :] for m in range(CONV_PB + CONV_WIDTH - 1)]
            for i in range(CONV_PB):
                acc = bias
                for k in range(CONV_WIDTH):
                    acc = acc + ev[i + k] * wv[k]
                y_ref[c, pl.ds(p0 + i, SUBLANES, stride=Y_PITCH), :] = acc
        if c % per_blk == per_blk - 1:
            yield


def _odd_out_stream(res_ref, h1_ref, lng_ref, lnb_ref, w2_ref, b2_ref, hn_ref, y_ref):
    d = jnp.concatenate(
        [jnp.concatenate([y_ref[c, pl.ds(s * Y_PITCH, SEG), :] for c in range(N_SLABS)], axis=1)
         for s in range(SUBLANES)], axis=0)
    mu = jnp.mean(d, axis=-1, keepdims=True)
    dc = d - mu
    var = jnp.mean(dc * dc, axis=-1, keepdims=True)
    zz = dc * lax.rsqrt(var + EPS) * lng_ref[...] + lnb_ref[...]
    hn_ref[...] = _swish(zz).astype(bf16)
    yield
    for c in range(D_MODEL // MXU_COLS):
        cols = pl.ds(c * MXU_COLS, MXU_COLS)
        h1_ref[:, cols] = res_ref[:, cols] + _dot(hn_ref[...], w2_ref[:, cols]) + b2_ref[:, cols]
        yield


N_FFN_IN = 5


def _emit(order, streams):
    for key in order:
        next(streams[key])
    for gen in streams.values():
        for _ in gen:
            pass


def _split_refs(refs, n_mix_in, n_cast):
    refs = list(refs)
    take = lambda k: [refs.pop(0) for _ in range(k)]
    (x_ref,), mix_in, ffn_in, cast_in = take(1), take(n_mix_in), take(N_FFN_IN), take(n_cast)
    (o_ref,), cast_out = take(1), take(n_cast)
    return x_ref, mix_in, ffn_in, o_ref, refs[:3], refs[3:], list(zip(cast_in, cast_out))


def _cast_blocks(cast):
    for src, dst in cast:
        dst[...] = src[...].astype(bf16)


def _even_kernel(*refs, n_mix_in, n_cast, tiles_per_seq, final, chunk_decay):
    x_ref, mix_in, ffn_in, o_ref, (h1_ref, hnf_ref, act_ref), mix_scratch, cast = _split_refs(
        refs, n_mix_in, n_cast)
    n = pl.program_id(0)
    tile_in_seq = lax.rem(n, tiles_per_seq)

    @pl.when(n == 0)
    def _():
        h1_ref[...] = jnp.zeros((TS, D_MODEL), f32)

    _even_init(tile_in_seq == 0, mix_scratch[2], mix_scratch[3])
    ffn = _ffn_stream(h1_ref, *ffn_in, hnf_ref, act_ref, o_ref, final)
    mixer = _even_stream(x_ref, h1_ref, tile_in_seq, *mix_in, *mix_scratch, chunk_decay)
    next(ffn)
    _cast_blocks(cast)
    n_ffn = ffn_in[1].shape[1] // MXU_COLS + D_MODEL // MXU_COLS
    n_head = mix_in[2].shape[1] // RET_WIDTH + 2 * (TS // CHUNK) + 1
    n_tail = D_MODEL // MXU_COLS
    _emit("MF" * n_head + "F" * (n_ffn - n_head) + "M" * n_tail, {"F": ffn, "M": mixer})


def _odd_kernel(*refs, n_mix_in, n_cast, tiles_per_seq, final):
    x_ref, mix_in, ffn_in, o_ref, (h1_ref, hnf_ref, act_ref), mix_scratch, cast = _split_refs(
        refs, n_mix_in, n_cast)
    ng_ref, w1_ref, b1_ref, wdw_ref, bdw_ref, lng_ref, lnb_ref, w2_ref, b2_ref = mix_in
    hn_ref, z_ref, e_ref, y_ref, carry_ref, hn2_ref = mix_scratch
    n = pl.program_id(0)

    @pl.when(n == 0)
    def _():
        h1_ref[...] = jnp.zeros((TS, D_MODEL), f32)

    _odd_init(lax.rem(n, tiles_per_seq) == 0, carry_ref)
    ffn = _ffn_stream(h1_ref, *ffn_in, hnf_ref, act_ref, o_ref, final)
    glu = _odd_glu_stream(x_ref, ng_ref, w1_ref, b1_ref, hn_ref, z_ref, carry_ref)
    conv = _odd_dwconv_stream(wdw_ref, bdw_ref, z_ref, e_ref, y_ref)
    out = _odd_out_stream(x_ref.at[0], h1_ref, lng_ref, lnb_ref, w2_ref, b2_ref, hn2_ref, y_ref)
    next(ffn)
    _cast_blocks(cast)
    n_gate_up = ffn_in[1].shape[1] // MXU_COLS
    n_blk = D_MODEL // MXU_COLS
    order = "GFF" + "GFFF" * (n_blk - 1) + "G" + "F" * n_blk + "C" * n_blk + "O" * (n_blk + 1)
    assert (order.count("F"), order.count("G")) == (n_gate_up + n_blk, n_blk + 1)
    _emit(order, {"F": ffn, "G": glu, "C": conv, "O": out})


def _whole(arr):
    zeros = (0,) * arr.ndim
    return arr, pl.BlockSpec(arr.shape, lambda n: zeros, pipeline_mode=pl.Buffered(1))


def _stacked(arr, idx):
    if arr.ndim == 2:
        arr = arr[:, None, :]
    zeros = (0,) * (arr.ndim - 1)
    return arr, pl.BlockSpec((None,) + arr.shape[1:], lambda n: (idx,) + zeros, pipeline_mode=pl.Buffered(1))


def _cast_specs(arr, idx, n_steps):
    _, R, C = arr.shape
    rows = next(r for r in range(2 * SUBLANES, R + 1, 2 * SUBLANES) if R % r == 0 and R // r <= n_steps)
    last = R // rows - 1
    return (pl.BlockSpec((None, rows, C), lambda n: (idx, jnp.minimum(n, last), 0)),
            pl.BlockSpec((rows, C), lambda n: (jnp.minimum(n, last), 0)),
            jax.ShapeDtypeStruct((R, C), bf16))


def _layer(kind, h, mix, mix_scratch, ffn, cast, final):
    B, S, D = h.shape
    nt = B * S // TS
    d_ff = ffn[1][0].shape[-1]
    last = nt - 1
    tile_in = lambda n: (jnp.minimum(n, last), 0, 0)
    tile_out = lambda n: (jnp.maximum(n - 1, 0), 0, 0)
    static = dict(n_mix_in=len(mix), n_cast=len(cast), tiles_per_seq=S // TS, final=final)
    if kind == "even":
        body = functools.partial(_even_kernel, chunk_decay=_chunk_decay(), **static)
    else:
        body = functools.partial(_odd_kernel, **static)
    mix_specs = [pl.BlockSpec((1, TS, a.shape[-1]), tile_in) if s == "tile" else s for a, s in mix]
    cast_specs = [_cast_specs(a, idx, nt + 1) for a, idx in cast]
    outs = pl.pallas_call(
        body,
        grid=(nt + 1,),
        in_specs=[pl.BlockSpec((1, TS, D), tile_in)] + mix_specs + [s for _, s in ffn]
        + [s[0] for s in cast_specs],
        out_specs=[pl.BlockSpec((1, TS, D), tile_out)] + [s[1] for s in cast_specs],
        out_shape=[jax.ShapeDtypeStruct((nt, TS, D), h.dtype)] + [s[2] for s in cast_specs],
        scratch_shapes=[pltpu.VMEM((TS, D), f32), pltpu.VMEM((TS, D), bf16), pltpu.VMEM((TS, d_ff), bf16)]
        + mix_scratch,
        compiler_params=pltpu.CompilerParams(dimension_semantics=("arbitrary",),
                                             vmem_limit_bytes=VMEM_LIMIT_BYTES),
        name=kind + ("_layer_final" if final else "_layer"),
    )(h.reshape(nt, TS, D), *[a for a, _ in mix], *[a for a, _ in ffn], *[a for a, _ in cast])
    return outs[0].reshape(B, S, D), outs[1:]


def _even_layer(h, pos, mix_params, ffn, cast, final):
    B, S, D = h.shape
    in_width = mix_params[1][0].shape[-1]
    half = HEAD_DIM // 2
    inv_freq = ROPE_BASE ** (-jnp.arange(half, dtype=f32) / half)
    inv_freq = jnp.concatenate([inv_freq, inv_freq])[None, :]
    sign = jnp.concatenate([-jnp.ones((half,), f32), jnp.ones((half,), f32)])[None, :]
    norm_g, w_in, ret_norm_g, pool_w, pool_scale, w_out = mix_params
    mix = [(pos, "tile"), norm_g, w_in, _whole(inv_freq), _whole(sign)]
    mix += [_whole(jnp.asarray(c)) for c in _retention_constants()]
    mix += [ret_norm_g, pool_w, pool_scale, w_out]
    mix_scratch = [pltpu.VMEM((TS, D), bf16),
                   pltpu.VMEM((TS, in_width), f32),
                   pltpu.VMEM((N_HEADS, HEAD_DIM, HEAD_DIM), f32),
                   pltpu.VMEM((POOL_HALO + TS, POOL_WIDTH), f32),
                   pltpu.VMEM((TS, RET_WIDTH + POOL_WIDTH), bf16)]
    return _layer("even", h, mix, mix_scratch, ffn, cast, final)


def _odd_layer(h, mix_params, ffn, cast, final):
    B, S, D = h.shape
    mix_scratch = [pltpu.VMEM((TS, D), bf16),
                   pltpu.VMEM((N_SLABS, SUBLANES * Z_PITCH, LANES), f32),
                   pltpu.VMEM((N_SLABS, (SEG + CONV_WIDTH - 1) * SUBLANES, LANES), f32),
                   pltpu.VMEM((N_SLABS, SUBLANES * Y_PITCH, LANES), f32),
                   pltpu.VMEM((N_SLABS, HALO, LANES), f32),
                   pltpu.VMEM((TS, D), bf16)]
    return _layer("odd", h, list(mix_params), mix_scratch, ffn, cast, final)


def kernel(x, positions, mixer_norm_g, ffn_norm_g, final_norm_g, ret_w_in, ret_norm_g, pool_w, pool_scale, mix_w_out, conv_w_pw1, conv_b_pw1, conv_w_dw, conv_b_dw, conv_ln_g, conv_ln_b, conv_w_pw2, conv_b_pw2, ffn_w_gate, ffn_w_up, ffn_w_down):
    B, S, D = x.shape
    depth = mixer_norm_g.shape[0]
    nt = B * S // TS
    pos = jnp.broadcast_to(positions.reshape(nt, TS, 1), (nt, TS, HEAD_DIM))
    w_pool = pool_w.astype(bf16)

    def big_weights(layer):
        i = layer // 2
        mixer = [(ret_w_in, i), (mix_w_out, i)] if layer % 2 == 0 else [(conv_w_pw1, i), (conv_w_pw2, i)]
        return [(ffn_w_gate, layer), (ffn_w_up, layer), (ffn_w_down, layer)] + mixer

    weights = [a[idx].astype(bf16) for a, idx in big_weights(0)]
    h = x
    for layer in range(depth):
        i = layer // 2
        final = layer == depth - 1
        w_gate, w_up, w_down, w_mix_a, w_mix_b = [_whole(w) for w in weights]
        ffn = [_stacked(ffn_norm_g, layer), w_gate, w_up, w_down, _whole(final_norm_g[None, :])]
        cast = [] if final else big_weights(layer + 1)
        if layer % 2 == 0:
            mix_params = [_stacked(mixer_norm_g, layer), w_mix_a, _stacked(ret_norm_g, i),
                          _stacked(w_pool, i), _stacked(pool_scale, i), w_mix_b]
            h, weights = _even_layer(h, pos, mix_params, ffn, cast, final)
        else:
            mix_params = [_stacked(mixer_norm_g, layer), w_mix_a, _stacked(conv_b_pw1, i),
                          _stacked(conv_w_dw, i), _stacked(conv_b_dw, i), _stacked(conv_ln_g, i),
                          _stacked(conv_ln_b, i), w_mix_b, _stacked(conv_b_pw2, i)]
            h, weights = _odd_layer(h, mix_params, ffn, cast, final)
    return h
```

```python
import functools

import numpy as np
import jax
import jax.numpy as jnp
from jax import lax
from jax.experimental import pallas as pl
from jax.experimental.pallas import tpu as pltpu

D_MODEL = 1024
N_HEADS = 4
HEAD_DIM = 128
RET_WIDTH = N_HEADS * HEAD_DIM
POOL_WINDOWS = (2, 4, 8, 16)
POOL_GROUP = 128
POOL_WIDTH = len(POOL_WINDOWS) * POOL_GROUP
CHUNK = 128
ROPE_BASE = 10000.0
CONV_WIDTH = 31
EPS = 1e-6

LANES = 128
SUBLANES = 8
MXU_COLS = 256
VMEM_LIMIT_BYTES = 56 * 1024 * 1024

TS = 512

SEG = TS // SUBLANES
HALO = 32
Z_PITCH = 104
Y_PITCH = 72
N_SLABS = D_MODEL // LANES
CONV_PB = 4

POOL_HALO = 16

bf16 = jnp.bfloat16
f32 = jnp.float32


def _dot(a, b):
    return jnp.dot(a, b, preferred_element_type=f32)


def _swish(x):
    h = 0.5 * x
    return h + h * jnp.tanh(h)


def _glu(a, g):
    h = 0.5 * a
    return h + h * jnp.tanh(0.5 * g)


def _rms_norm(x, g):
    ms = jnp.mean(x * x, axis=-1, keepdims=True)
    return x * lax.rsqrt(ms + EPS) * g


def _ffn_stream(h1_ref, g_ref, wgu_ref, wd_ref, fg_ref, hn_ref, act_ref, o_ref, final):
    x = h1_ref[...]
    hn_ref[...] = _rms_norm(x, g_ref[...]).astype(bf16)
    o_ref[0] = x
    yield
    d_ff = wd_ref.shape[0]
    for c in range(d_ff // MXU_COLS):
        gu = _dot(hn_ref[...], wgu_ref[:, pl.ds(c * 2 * MXU_COLS, 2 * MXU_COLS)])
        act_ref[:, pl.ds(c * MXU_COLS, MXU_COLS)] = (_swish(gu[:, :MXU_COLS]) * gu[:, MXU_COLS:]).astype(bf16)
        yield
    for c in range(D_MODEL // MXU_COLS):
        cols = pl.ds(c * MXU_COLS, MXU_COLS)
        o_ref[0, :, cols] = o_ref[0, :, cols] + _dot(act_ref[...], wd_ref[:, cols])
        yield
    if final:
        o_ref[0] = _rms_norm(o_ref[0], fg_ref[...])


def _even_init(first, state_ref, uext_ref):
    @pl.when(first)
    def _():
        state_ref[...] = jnp.zeros_like(state_ref)
        uext_ref[pl.ds(0, POOL_HALO), :] = jnp.zeros((POOL_HALO, POOL_WIDTH), f32)


def _even_stream(x_ref, h1_ref, tile_in_seq, pos_ref, ng_ref, win_ref, invf_ref, sign_ref, mask_ref,
                 qdec_ref, kdec_ref, rng_ref, pw_ref, ps_ref, wout_ref,
                 hn_ref, proj_ref, state_ref, uext_ref, mix_ref, chunk_decay):
    hn_ref[...] = _rms_norm(x_ref[0], ng_ref[...]).astype(bf16)
    in_width = win_ref.shape[1]
    for part in range(in_width // RET_WIDTH):
        cols = pl.ds(part * RET_WIDTH, RET_WIDTH)
        proj_ref[:, cols] = _dot(hn_ref[...], win_ref[:, cols])
        yield

    ang = pos_ref[0].astype(f32) * invf_ref[...]
    cos_t = jnp.cos(ang)
    sin_t = jnp.sin(ang) * sign_ref[...]

    def rope(t, cs, sn):
        return t * cs + pltpu.roll(t, HEAD_DIM // 2, axis=1) * sn

    heads = range(N_HEADS)
    for c in range(TS // CHUNK):
        rows = pl.ds(c * CHUNK, CHUNK)
        cs = cos_t[c * CHUNK:(c + 1) * CHUNK]
        sn = sin_t[c * CHUNK:(c + 1) * CHUNK]
        q = [rope(proj_ref[rows, pl.ds(h * HEAD_DIM, HEAD_DIM)], cs, sn) for h in heads]
        k = [rope(proj_ref[rows, pl.ds(RET_WIDTH + h * HEAD_DIM, HEAD_DIM)], cs, sn) for h in heads]
        v = [proj_ref[rows, pl.ds(2 * RET_WIDTH + h * HEAD_DIM, HEAD_DIM)].astype(bf16) for h in heads]
        scores = [lax.dot_general(q[h].astype(bf16), k[h].astype(bf16), (((1,), (1,)), ((), ())),
                                  preferred_element_type=f32) for h in heads]
        yield
        o = []
        for h in heads:
            sm = (scores[h] * mask_ref[h]).astype(bf16)
            qd = (q[h] * qdec_ref[h]).astype(bf16)
            st = state_ref[h]
            o.append(_dot(jnp.concatenate([sm, qd], axis=1),
                          jnp.concatenate([v[h], st.astype(bf16)], axis=0)))
            kd = (k[h] * kdec_ref[h]).astype(bf16)
            kv = lax.dot_general(kd, v[h], (((0,), (0,)), ((), ())), preferred_element_type=f32)
            state_ref[h] = st * chunk_decay[h] + kv
        yield
        for h in heads:
            col = pl.ds(h * HEAD_DIM, HEAD_DIM)
            gate = proj_ref[rows, pl.ds(3 * RET_WIDTH + h * HEAD_DIM, HEAD_DIM)]
            mu = jnp.mean(o[h], axis=-1, keepdims=True)
            oc = o[h] - mu
            var = jnp.mean(oc * oc, axis=-1, keepdims=True)
            on = oc * lax.rsqrt(var + EPS) * rng_ref[:, col]
            mix_ref[rows, col] = (_swish(gate) * on).astype(bf16)

    ucols = pl.ds(4 * RET_WIDTH, POOL_WIDTH)
    uext_ref[pl.ds(POOL_HALO, TS), :] = proj_ref[:, ucols]
    row = lax.broadcasted_iota(jnp.int32, (TS, POOL_GROUP), 0) + tile_in_seq * TS + 1
    for gi, w in enumerate(POOL_WINDOWS):
        gcols = pl.ds(gi * POOL_GROUP, POOL_GROUP)
        e = uext_ref[:, gcols]
        s = e
        step = 1
        while step < w:
            s = s + pltpu.roll(s, step, axis=0)
            step *= 2
        cnt = jnp.minimum(row, w).astype(f32)
        y = s[POOL_HALO:] / cnt - e[POOL_HALO:]
        pooled = _dot(y.astype(bf16), pw_ref[gi]) * ps_ref[:, gcols]
        mix_ref[:, pl.ds(RET_WIDTH + gi * POOL_GROUP, POOL_GROUP)] = pooled.astype(bf16)
    uext_ref[pl.ds(0, POOL_HALO), :] = uext_ref[pl.ds(TS, POOL_HALO), :]
    yield

    for c in range(D_MODEL // MXU_COLS):
        cols = pl.ds(c * MXU_COLS, MXU_COLS)
        h1_ref[:, cols] = x_ref[0, :, cols] + _dot(mix_ref[...], wout_ref[:, cols])
        yield


def _retention_constants():
    idx = np.arange(CHUNK, dtype=np.float32)
    log_gamma = np.log1p(-(np.float32(2.0) ** (-5.0 - np.arange(N_HEADS, dtype=np.float32)))).astype(np.float32)
    rel = idx[:, None] - idx[None, :]
    mask = np.where(rel >= 0, np.exp(log_gamma[:, None, None] * np.maximum(rel, 0.0)), 0.0).astype(np.float32)
    scale = np.float32(HEAD_DIM ** -0.5)
    q_decay = np.exp(log_gamma[:, None] * (idx + 1.0)).astype(np.float32)
    k_decay = np.exp(log_gamma[:, None] * (CHUNK - 1.0 - idx)).astype(np.float32)
    ones = np.ones((1, 1, HEAD_DIM), np.float32)
    return (mask * scale, q_decay[:, :, None] * ones, (k_decay * scale)[:, :, None] * ones)


def _chunk_decay():
    lg = np.log1p(-(2.0 ** (-5.0 - np.arange(N_HEADS, dtype=np.float32)))).astype(np.float32)
    return tuple(float(v) for v in np.exp(lg * np.float32(CHUNK)).astype(np.float32))


def _odd_init(first, carry_ref):
    @pl.when(first)
    def _():
        carry_ref[...] = jnp.zeros_like(carry_ref)


def _odd_glu_stream(x_ref, ng_ref, w1_ref, b1_ref, hn_ref, z_ref, carry_ref):
    hn_ref[...] = _rms_norm(x_ref[0], ng_ref[...]).astype(bf16)
    per_blk = MXU_COLS // LANES
    for c2 in range(D_MODEL // MXU_COLS):
        cols = pl.ds(c2 * MXU_COLS, MXU_COLS)
        gcols = pl.ds(D_MODEL + c2 * MXU_COLS, MXU_COLS)
        a = _dot(hn_ref[...], w1_ref[:, cols]) + b1_ref[:, cols]
        gt = _dot(hn_ref[...], w1_ref[:, gcols]) + b1_ref[:, gcols]
        yield
        u2 = _glu(a, gt)
        for ci in range(per_blk):
            c = c2 * per_blk + ci
            uc = u2[:, ci * LANES:(ci + 1) * LANES]
            z_ref[c, pl.ds(0, HALO), :] = carry_ref[c]
            for s in range(SUBLANES):
                z_ref[c, pl.ds(s * Z_PITCH + HALO, SEG), :] = uc[s * SEG:(s + 1) * SEG]
                tail = uc[(s + 1) * SEG - HALO:(s + 1) * SEG]
                if s + 1 < SUBLANES:
                    z_ref[c, pl.ds((s + 1) * Z_PITCH, HALO), :] = tail
                else:
                    carry_ref[c] = tail
    yield


def _odd_dwconv_stream(wdw_ref, bdw_ref, z_ref, e_ref, y_ref):
    base = HALO - (CONV_WIDTH - 1)
    n_e = SEG + CONV_WIDTH - 1
    per_blk = MXU_COLS // LANES
    for c in range(N_SLABS):
        for m in range(n_e):
            e_ref[c, pl.ds(m * SUBLANES, SUBLANES), :] = z_ref[c, pl.ds(base + m, SUBLANES, stride=Z_PITCH), :]
        lanes = pl.ds(c * LANES, LANES)
        wv = [jnp.broadcast_to(wdw_ref[pl.ds(k, 1), lanes], (SUBLANES, LANES)) for k in range(CONV_WIDTH)]
        bias = jnp.broadcast_to(bdw_ref[:, lanes], (SUBLANES, LANES))
        for p0 in range(0, SEG, CONV_PB):
            ev = [e_ref[c, pl.ds((p0 + m) * SUBLANES, SUBLANES), :] for m in range(CONV_PB + CONV_WIDTH - 1)]
            for i in range(CONV_PB):
                acc = bias
                for k in range(CONV_WIDTH):
                    acc = acc + ev[i + k] * wv[k]
                y_ref[c, pl.ds(p0 + i, SUBLANES, stride=Y_PITCH), :] = acc
        if c % per_blk == per_blk - 1:
            yield


def _odd_out_stream(res_ref, h1_ref, lng_ref, lnb_ref, w2_ref, b2_ref, hn_ref, y_ref):
    d = jnp.concatenate(
        [jnp.concatenate([y_ref[c, pl.ds(s * Y_PITCH, SEG), :] for c in range(N_SLABS)], axis=1)
         for s in range(SUBLANES)], axis=0)
    mu = jnp.mean(d, axis=-1, keepdims=True)
    dc = d - mu
    var = jnp.mean(dc * dc, axis=-1, keepdims=True)
    zz = dc * lax.rsqrt(var + EPS) * lng_ref[...] + lnb_ref[...]
    hn_ref[...] = _swish(zz).astype(bf16)
    yield
    for c in range(D_MODEL // MXU_COLS):
        cols = pl.ds(c * MXU_COLS, MXU_COLS)
        h1_ref[:, cols] = res_ref[:, cols] + _dot(hn_ref[...], w2_ref[:, cols]) + b2_ref[:, cols]
        yield


N_FFN_IN = 4


def _emit(order, streams):
    for key in order:
        next(streams[key])
    for gen in streams.values():
        for _ in gen:
            pass


def _split_refs(refs, n_mix_in, n_cast):
    refs = list(refs)
    take = lambda k: [refs.pop(0) for _ in range(k)]
    (x_ref,), mix_in, ffn_in = take(1), take(n_mix_in), take(N_FFN_IN)
    cast_in = [take(k) for k in n_cast]
    (o_ref,), cast_out = take(1), take(len(n_cast))
    return x_ref, mix_in, ffn_in, o_ref, refs[:3], refs[3:], list(zip(cast_in, cast_out))


def _cast_blocks(cast):
    for srcs, dst in cast:
        k = len(srcs)
        for c in range(srcs[0].shape[1] // MXU_COLS if k > 1 else 0):
            for j, src in enumerate(srcs):
                dst[:, pl.ds((c * k + j) * MXU_COLS, MXU_COLS)] = src[:, pl.ds(c * MXU_COLS, MXU_COLS)].astype(bf16)
        if k == 1:
            dst[...] = srcs[0][...].astype(bf16)


def _even_kernel(*refs, n_mix_in, n_cast, tiles_per_seq, final, chunk_decay):
    x_ref, mix_in, ffn_in, o_ref, (h1_ref, hnf_ref, act_ref), mix_scratch, cast = _split_refs(
        refs, n_mix_in, n_cast)
    n = pl.program_id(0)
    tile_in_seq = lax.rem(n, tiles_per_seq)

    @pl.when(n == 0)
    def _():
        h1_ref[...] = jnp.zeros((TS, D_MODEL), f32)

    _even_init(tile_in_seq == 0, mix_scratch[2], mix_scratch[3])
    ffn = _ffn_stream(h1_ref, *ffn_in, hnf_ref, act_ref, o_ref, final)
    mixer = _even_stream(x_ref, h1_ref, tile_in_seq, *mix_in, *mix_scratch, chunk_decay)
    next(ffn)
    _cast_blocks(cast)
    n_ffn = ffn_in[2].shape[0] // MXU_COLS + D_MODEL // MXU_COLS
    n_head = mix_in[2].shape[1] // RET_WIDTH + 2 * (TS // CHUNK) + 1
    n_tail = D_MODEL // MXU_COLS
    _emit("MF" * n_head + "F" * (n_ffn - n_head) + "M" * n_tail, {"F": ffn, "M": mixer})


def _odd_kernel(*refs, n_mix_in, n_cast, tiles_per_seq, final):
    x_ref, mix_in, ffn_in, o_ref, (h1_ref, hnf_ref, act_ref), mix_scratch, cast = _split_refs(
        refs, n_mix_in, n_cast)
    ng_ref, w1_ref, b1_ref, wdw_ref, bdw_ref, lng_ref, lnb_ref, w2_ref, b2_ref = mix_in
    hn_ref, z_ref, e_ref, y_ref, carry_ref, hn2_ref = mix_scratch
    n = pl.program_id(0)

    @pl.when(n == 0)
    def _():
        h1_ref[...] = jnp.zeros((TS, D_MODEL), f32)

    _odd_init(lax.rem(n, tiles_per_seq) == 0, carry_ref)
    ffn = _ffn_stream(h1_ref, *ffn_in, hnf_ref, act_ref, o_ref, final)
    glu = _odd_glu_stream(x_ref, ng_ref, w1_ref, b1_ref, hn_ref, z_ref, carry_ref)
    conv = _odd_dwconv_stream(wdw_ref, bdw_ref, z_ref, e_ref, y_ref)
    out = _odd_out_stream(x_ref.at[0], h1_ref, lng_ref, lnb_ref, w2_ref, b2_ref, hn2_ref, y_ref)
    next(ffn)
    _cast_blocks(cast)
    n_gate_up = ffn_in[2].shape[0] // MXU_COLS
    n_blk = D_MODEL // MXU_COLS
    order = "GFF" + "GFFF" * (n_blk - 1) + "G" + "F" * n_blk + "C" * n_blk + "O" * (n_blk + 1)
    assert (order.count("F"), order.count("G")) == (n_gate_up + n_blk, n_blk + 1)
    _emit(order, {"F": ffn, "G": glu, "C": conv, "O": out})


def _whole(arr):
    zeros = (0,) * arr.ndim
    return arr, pl.BlockSpec(arr.shape, lambda n: zeros, pipeline_mode=pl.Buffered(1))


def _stacked(arr, idx):
    if arr.ndim == 2:
        arr = arr[:, None, :]
    zeros = (0,) * (arr.ndim - 1)
    return arr, pl.BlockSpec((None,) + arr.shape[1:], lambda n: (idx,) + zeros, pipeline_mode=pl.Buffered(1))


def _cast_specs(arrs, idx, n_steps):
    _, R, C = arrs[0].shape
    rows = next(r for r in range(2 * SUBLANES, R + 1, 2 * SUBLANES) if R % r == 0 and R // r <= n_steps)
    last = R // rows - 1
    return ([pl.BlockSpec((None, rows, C), lambda n: (idx, jnp.minimum(n, last), 0)) for _ in arrs],
            pl.BlockSpec((rows, len(arrs) * C), lambda n: (jnp.minimum(n, last), 0)),
            jax.ShapeDtypeStruct((R, len(arrs) * C), bf16))


def _layer(kind, h, mix, mix_scratch, ffn, cast, final):
    B, S, D = h.shape
    nt = B * S // TS
    d_ff = ffn[2][0].shape[0]
    last = nt - 1
    tile_in = lambda n: (jnp.minimum(n, last), 0, 0)
    tile_out = lambda n: (jnp.maximum(n - 1, 0), 0, 0)
    static = dict(n_mix_in=len(mix), n_cast=tuple(len(a) for a, _ in cast), tiles_per_seq=S // TS, final=final)
    if kind == "even":
        body = functools.partial(_even_kernel, chunk_decay=_chunk_decay(), **static)
    else:
        body = functools.partial(_odd_kernel, **static)
    mix_specs = [pl.BlockSpec((1, TS, a.shape[-1]), tile_in) if s == "tile" else s for a, s in mix]
    cast_specs = [_cast_specs(arrs, idx, nt + 1) for arrs, idx in cast]
    outs = pl.pallas_call(
        body,
        grid=(nt + 1,),
        in_specs=[pl.BlockSpec((1, TS, D), tile_in)] + mix_specs + [s for _, s in ffn]
        + [sp for s in cast_specs for sp in s[0]],
        out_specs=[pl.BlockSpec((1, TS, D), tile_out)] + [s[1] for s in cast_specs],
        out_shape=[jax.ShapeDtypeStruct((nt, TS, D), h.dtype)] + [s[2] for s in cast_specs],
        scratch_shapes=[pltpu.VMEM((TS, D), f32), pltpu.VMEM((TS, D), bf16), pltpu.VMEM((TS, d_ff), bf16)]
        + mix_scratch,
        compiler_params=pltpu.CompilerParams(dimension_semantics=("arbitrary",),
                                             vmem_limit_bytes=VMEM_LIMIT_BYTES),
        name=kind + ("_layer_final" if final else "_layer"),
    )(h.reshape(nt, TS, D), *[a for a, _ in mix], *[a for a, _ in ffn], *[a for arrs, _ in cast for a in arrs])
    return outs[0].reshape(B, S, D), outs[1:]


def _even_layer(h, pos, mix_params, ffn, cast, final):
    B, S, D = h.shape
    in_width = mix_params[1][0].shape[-1]
    half = HEAD_DIM // 2
    inv_freq = ROPE_BASE ** (-jnp.arange(half, dtype=f32) / half)
    inv_freq = jnp.concatenate([inv_freq, inv_freq])[None, :]
    sign = jnp.concatenate([-jnp.ones((half,), f32), jnp.ones((half,), f32)])[None, :]
    norm_g, w_in, ret_norm_g, pool_w, pool_scale, w_out = mix_params
    mix = [(pos, "tile"), norm_g, w_in, _whole(inv_freq), _whole(sign)]
    mix += [_whole(jnp.asarray(c)) for c in _retention_constants()]
    mix += [ret_norm_g, pool_w, pool_scale, w_out]
    mix_scratch = [pltpu.VMEM((TS, D), bf16),
                   pltpu.VMEM((TS, in_width), f32),
                   pltpu.VMEM((N_HEADS, HEAD_DIM, HEAD_DIM), f32),
                   pltpu.VMEM((POOL_HALO + TS, POOL_WIDTH), f32),
                   pltpu.VMEM((TS, RET_WIDTH + POOL_WIDTH), bf16)]
    return _layer("even", h, mix, mix_scratch, ffn, cast, final)


def _odd_layer(h, mix_params, ffn, cast, final):
    B, S, D = h.shape
    mix_scratch = [pltpu.VMEM((TS, D), bf16),
                   pltpu.VMEM((N_SLABS, SUBLANES * Z_PITCH, LANES), f32),
                   pltpu.VMEM((N_SLABS, (SEG + CONV_WIDTH - 1) * SUBLANES, LANES), f32),
                   pltpu.VMEM((N_SLABS, SUBLANES * Y_PITCH, LANES), f32),
                   pltpu.VMEM((N_SLABS, HALO, LANES), f32),
                   pltpu.VMEM((TS, D), bf16)]
    return _layer("odd", h, list(mix_params), mix_scratch, ffn, cast, final)


def kernel(x, positions, mixer_norm_g, ffn_norm_g, final_norm_g, ret_w_in, ret_norm_g, pool_w, pool_scale, mix_w_out, conv_w_pw1, conv_b_pw1, conv_w_dw, conv_b_dw, conv_ln_g, conv_ln_b, conv_w_pw2, conv_b_pw2, ffn_w_gate, ffn_w_up, ffn_w_down):
    B, S, D = x.shape
    depth = mixer_norm_g.shape[0]
    nt = B * S // TS
    pos = jnp.broadcast_to(positions.reshape(nt, TS, 1), (nt, TS, HEAD_DIM))
    w_pool = pool_w.astype(bf16)

    def big_weights(layer):
        i = layer // 2
        mixer = [([ret_w_in], i), ([mix_w_out], i)] if layer % 2 == 0 else [([conv_w_pw1], i), ([conv_w_pw2], i)]
        return [([ffn_w_gate, ffn_w_up], layer), ([ffn_w_down], layer)] + mixer

    def cast_here(arrs, idx):
        w = jnp.stack([a[idx].reshape(a.shape[1], -1, MXU_COLS) for a in arrs], axis=2)
        return w.reshape(w.shape[0], -1).astype(bf16)

    weights = [cast_here(arrs, idx) for arrs, idx in big_weights(0)]
    h = x
    for layer in range(depth):
        i = layer // 2
        final = layer == depth - 1
        w_gate_up, w_down, w_mix_a, w_mix_b = [_whole(w) for w in weights]
        ffn = [_stacked(ffn_norm_g, layer), w_gate_up, w_down, _whole(final_norm_g[None, :])]
        cast = [] if final else big_weights(layer + 1)
        if layer % 2 == 0:
            mix_params = [_stacked(mixer_norm_g, layer), w_mix_a, _stacked(ret_norm_g, i),
                          _stacked(w_pool, i), _stacked(pool_scale, i), w_mix_b]
            h, weights = _even_layer(h, pos, mix_params, ffn, cast, final)
        else:
            mix_params = [_stacked(mixer_norm_g, layer), w_mix_a, _stacked(conv_b_pw1, i),
                          _stacked(conv_w_dw, i), _stacked(conv_b_dw, i), _stacked(conv_ln_g, i),
                          _stacked(conv_ln_b, i), w_mix_b, _stacked(conv_b_pw2, i)]
            h, weights = _odd_layer(h, mix_params, ffn, cast, final)
    return h
```

```python
import functools

import numpy as np
import jax
import jax.numpy as jnp
from jax import lax
from jax.experimental import pallas as pl
from jax.experimental.pallas import tpu as pltpu

D_MODEL = 1024
N_HEADS = 4
HEAD_DIM = 128
RET_WIDTH = N_HEADS * HEAD_DIM
POOL_WINDOWS = (2, 4, 8, 16)
POOL_GROUP = 128
POOL_WIDTH = len(POOL_WINDOWS) * POOL_GROUP
CHUNK = 128
ROPE_BASE = 10000.0
CONV_WIDTH = 31
EPS = 1e-6

LANES = 128
SUBLANES = 8
MXU_COLS = 256
VMEM_LIMIT_BYTES = 56 * 1024 * 1024

TS = 512

SEG = TS // SUBLANES
HALO = 32
Z_PITCH = 104
Y_PITCH = 72
N_SLABS = D_MODEL // LANES
CONV_PB = 4

POOL_HALO = 16

bf16 = jnp.bfloat16
f32 = jnp.float32


def _dot(a, b):
    return jnp.dot(a, b, preferred_element_type=f32)


def _swish(x):
    h = 0.5 * x
    return h + h * jnp.tanh(h)


def _glu(a, g):
    h = 0.5 * a
    return h + h * jnp.tanh(0.5 * g)


def _rms_norm(x, g):
    ms = jnp.mean(x * x, axis=-1, keepdims=True)
    return x * lax.rsqrt(ms + EPS) * g


def _ffn_stream(h1_ref, g_ref, wg_ref, wu_ref, wd_ref, fg_ref, hn_ref, act_ref, o_ref, final):
    hn_ref[...] = _rms_norm(h1_ref[...], g_ref[...]).astype(bf16)
    yield
    d_ff = wg_ref.shape[1]
    for c in range(d_ff // MXU_COLS):
        cols = pl.ds(c * MXU_COLS, MXU_COLS)
        gate = _dot(hn_ref[...], wg_ref[:, cols])
        up = _dot(hn_ref[...], wu_ref[:, cols])
        act_ref[:, cols] = (_swish(gate) * up).astype(bf16)
        yield
    for c in range(D_MODEL // MXU_COLS):
        cols = pl.ds(c * MXU_COLS, MXU_COLS)
        o_ref[0, :, cols] = h1_ref[:, cols] + _dot(act_ref[...], wd_ref[:, cols])
        yield
    if final:
        o_ref[0] = _rms_norm(o_ref[0], fg_ref[...])


def _even_init(first, state_ref, uext_ref):
    @pl.when(first)
    def _():
        state_ref[...] = jnp.zeros_like(state_ref)
        uext_ref[pl.ds(0, POOL_HALO), :] = jnp.zeros((POOL_HALO, POOL_WIDTH), f32)


def _even_stream(x_ref, h1_ref, tile_in_seq, pos_ref, ng_ref, win_ref, invf_ref, sign_ref, mask_ref,
                 qdec_ref, kdec_ref, rng_ref, pw_ref, ps_ref, wout_ref,
                 hn_ref, proj_ref, state_ref, uext_ref, mix_ref, chunk_decay):
    hn_ref[...] = _rms_norm(x_ref[0], ng_ref[...]).astype(bf16)
    in_width = win_ref.shape[1]
    for part in range(in_width // RET_WIDTH):
        cols = pl.ds(part * RET_WIDTH, RET_WIDTH)
        proj_ref[:, cols] = _dot(hn_ref[...], win_ref[:, cols])
        yield

    ang = pos_ref[0].astype(f32) * invf_ref[...]
    cos_t = jnp.cos(ang)
    sin_t = jnp.sin(ang) * sign_ref[...]

    def rope(t, cs, sn):
        return t * cs + pltpu.roll(t, HEAD_DIM // 2, axis=1) * sn

    heads = range(N_HEADS)
    for c in range(TS // CHUNK):
        rows = pl.ds(c * CHUNK, CHUNK)
        cs = cos_t[c * CHUNK:(c + 1) * CHUNK]
        sn = sin_t[c * CHUNK:(c + 1) * CHUNK]
        q = [rope(proj_ref[rows, pl.ds(h * HEAD_DIM, HEAD_DIM)], cs, sn) for h in heads]
        k = [rope(proj_ref[rows, pl.ds(RET_WIDTH + h * HEAD_DIM, HEAD_DIM)], cs, sn) for h in heads]
        v = [proj_ref[rows, pl.ds(2 * RET_WIDTH + h * HEAD_DIM, HEAD_DIM)].astype(bf16) for h in heads]
        scores = [lax.dot_general(q[h].astype(bf16), k[h].astype(bf16), (((1,), (1,)), ((), ())),
                                  preferred_element_type=f32) for h in heads]
        yield
        o = []
        for h in heads:
            sm = (scores[h] * mask_ref[h]).astype(bf16)
            qd = (q[h] * qdec_ref[h]).astype(bf16)
            st = state_ref[h]
            o.append(_dot(jnp.concatenate([sm, qd], axis=1),
                          jnp.concatenate([v[h], st.astype(bf16)], axis=0)))
            kd = (k[h] * kdec_ref[h]).astype(bf16)
            kv = lax.dot_general(kd, v[h], (((0,), (0,)), ((), ())), preferred_element_type=f32)
            state_ref[h] = st * chunk_decay[h] + kv
        yield
        for h in heads:
            col = pl.ds(h * HEAD_DIM, HEAD_DIM)
            gate = proj_ref[rows, pl.ds(3 * RET_WIDTH + h * HEAD_DIM, HEAD_DIM)]
            mu = jnp.mean(o[h], axis=-1, keepdims=True)
            oc = o[h] - mu
            var = jnp.mean(oc * oc, axis=-1, keepdims=True)
            on = oc * lax.rsqrt(var + EPS) * rng_ref[:, col]
            mix_ref[rows, col] = (_swish(gate) * on).astype(bf16)

    ucols = pl.ds(4 * RET_WIDTH, POOL_WIDTH)
    uext_ref[pl.ds(POOL_HALO, TS), :] = proj_ref[:, ucols]
    row = lax.broadcasted_iota(jnp.int32, (TS, POOL_GROUP), 0) + tile_in_seq * TS + 1
    for gi, w in enumerate(POOL_WINDOWS):
        gcols = pl.ds(gi * POOL_GROUP, POOL_GROUP)
        e = uext_ref[:, gcols]
        s = e
        step = 1
        while step < w:
            s = s + pltpu.roll(s, step, axis=0)
            step *= 2
        cnt = jnp.minimum(row, w).astype(f32)
        y = s[POOL_HALO:] / cnt - e[POOL_HALO:]
        pooled = _dot(y.astype(bf16), pw_ref[gi]) * ps_ref[:, gcols]
        mix_ref[:, pl.ds(RET_WIDTH + gi * POOL_GROUP, POOL_GROUP)] = pooled.astype(bf16)
    uext_ref[pl.ds(0, POOL_HALO), :] = uext_ref[pl.ds(TS, POOL_HALO), :]
    yield

    for c in range(D_MODEL // MXU_COLS):
        cols = pl.ds(c * MXU_COLS, MXU_COLS)
        h1_ref[:, cols] = x_ref[0, :, cols] + _dot(mix_ref[...], wout_ref[:, cols])
        yield


def _retention_constants():
    idx = np.arange(CHUNK, dtype=np.float32)
    log_gamma = np.log1p(-(np.float32(2.0) ** (-5.0 - np.arange(N_HEADS, dtype=np.float32)))).astype(np.float32)
    rel = idx[:, None] - idx[None, :]
    mask = np.where(rel >= 0, np.exp(log_gamma[:, None, None] * np.maximum(rel, 0.0)), 0.0).astype(np.float32)
    scale = np.float32(HEAD_DIM ** -0.5)
    q_decay = np.exp(log_gamma[:, None] * (idx + 1.0)).astype(np.float32)
    k_decay = np.exp(log_gamma[:, None] * (CHUNK - 1.0 - idx)).astype(np.float32)
    ones = np.ones((1, 1, HEAD_DIM), np.float32)
    return (mask * scale, q_decay[:, :, None] * ones, (k_decay * scale)[:, :, None] * ones)


def _chunk_decay():
    lg = np.log1p(-(2.0 ** (-5.0 - np.arange(N_HEADS, dtype=np.float32)))).astype(np.float32)
    return tuple(float(v) for v in np.exp(lg * np.float32(CHUNK)).astype(np.float32))


def _odd_init(first, carry_ref):
    @pl.when(first)
    def _():
        carry_ref[...] = jnp.zeros_like(carry_ref)


def _odd_glu_stream(x_ref, ng_ref, w1_ref, b1_ref, hn_ref, z_ref, carry_ref):
    hn_ref[...] = _rms_norm(x_ref[0], ng_ref[...]).astype(bf16)
    per_blk = MXU_COLS // LANES
    for c2 in range(D_MODEL // MXU_COLS):
        cols = pl.ds(c2 * MXU_COLS, MXU_COLS)
        gcols = pl.ds(D_MODEL + c2 * MXU_COLS, MXU_COLS)
        a = _dot(hn_ref[...], w1_ref[:, cols]) + b1_ref[:, cols]
        gt = _dot(hn_ref[...], w1_ref[:, gcols]) + b1_ref[:, gcols]
        yield
        u2 = _glu(a, gt)
        for ci in range(per_blk):
            c = c2 * per_blk + ci
            uc = u2[:, ci * LANES:(ci + 1) * LANES]
            z_ref[c, pl.ds(0, HALO), :] = carry_ref[c]
            for s in range(SUBLANES):
                z_ref[c, pl.ds(s * Z_PITCH + HALO, SEG), :] = uc[s * SEG:(s + 1) * SEG]
                tail = uc[(s + 1) * SEG - HALO:(s + 1) * SEG]
                if s + 1 < SUBLANES:
                    z_ref[c, pl.ds((s + 1) * Z_PITCH, HALO), :] = tail
                else:
                    carry_ref[c] = tail
    yield


def _odd_dwconv_stream(wdw_ref, bdw_ref, z_ref, e_ref, y_ref):
    base = HALO - (CONV_WIDTH - 1)
    n_e = SEG + CONV_WIDTH - 1
    per_blk = MXU_COLS // LANES
    for c in range(N_SLABS):
        for m in range(n_e):
            e_ref[c, pl.ds(m * SUBLANES, SUBLANES), :] = z_ref[c, pl.ds(base + m, SUBLANES, stride=Z_PITCH), :]
        lanes = pl.ds(c * LANES, LANES)
        wv = [jnp.broadcast_to(wdw_ref[pl.ds(k, 1), lanes], (SUBLANES, LANES)) for k in range(CONV_WIDTH)]
        bias = jnp.broadcast_to(bdw_ref[:, lanes], (SUBLANES, LANES))
        for p0 in range(0, SEG, CONV_PB):
            ev = [e_ref[c, pl.ds((p0 + m) * SUBLANES, SUBLANES), :] for m in range(CONV_PB + CONV_WIDTH - 1)]
            for i in range(CONV_PB):
                acc = bias
                for k in range(CONV_WIDTH):
                    acc = acc + ev[i + k] * wv[k]
                y_ref[c, pl.ds(p0 + i, SUBLANES, stride=Y_PITCH), :] = acc
        if c % per_blk == per_blk - 1:
            yield


def _odd_out_stream(res_ref, h1_ref, lng_ref, lnb_ref, w2_ref, b2_ref, hn_ref, y_ref):
    d = jnp.concatenate(
        [jnp.concatenate([y_ref[c, pl.ds(s * Y_PITCH, SEG), :] for c in range(N_SLABS)], axis=1)
         for s in range(SUBLANES)], axis=0)
    mu = jnp.mean(d, axis=-1, keepdims=True)
    dc = d - mu
    var = jnp.mean(dc * dc, axis=-1, keepdims=True)
    zz = dc * lax.rsqrt(var + EPS) * lng_ref[...] + lnb_ref[...]
    hn_ref[...] = _swish(zz).astype(bf16)
    yield
    for c in range(D_MODEL // MXU_COLS):
        cols = pl.ds(c * MXU_COLS, MXU_COLS)
        h1_ref[:, cols] = res_ref[:, cols] + _dot(hn_ref[...], w2_ref[:, cols]) + b2_ref[:, cols]
        yield


N_FFN_IN = 5


def _emit(order, streams):
    for key in order:
        next(streams[key])
    for gen in streams.values():
        for _ in gen:
            pass


def _split_refs(refs, n_mix_in, n_cast):
    refs = list(refs)
    take = lambda k: [refs.pop(0) for _ in range(k)]
    (x_ref,), mix_in, ffn_in, cast_in = take(1), take(n_mix_in), take(N_FFN_IN), take(n_cast)
    (o_ref,), cast_out = take(1), take(n_cast)
    return x_ref, mix_in, ffn_in, o_ref, refs[:3], refs[3:], list(zip(cast_in, cast_out))


def _cast_blocks(cast):
    for src, dst in cast:
        dst[...] = src[...].astype(bf16)


def _even_kernel(*refs, n_mix_in, n_cast, tiles_per_seq, final, chunk_decay):
    x_ref, mix_in, ffn_in, o_ref, (h1_ref, hnf_ref, act_ref), mix_scratch, cast = _split_refs(
        refs, n_mix_in, n_cast)
    n = pl.program_id(0)
    tile_in_seq = lax.rem(n, tiles_per_seq)

    @pl.when(n == 0)
    def _():
        h1_ref[...] = jnp.zeros((TS, D_MODEL), f32)

    _even_init(tile_in_seq == 0, mix_scratch[2], mix_scratch[3])
    ffn = _ffn_stream(h1_ref, *ffn_in, hnf_ref, act_ref, o_ref, final)
    mixer = _even_stream(x_ref, h1_ref, tile_in_seq, *mix_in, *mix_scratch, chunk_decay)
    next(ffn)
    _cast_blocks(cast)
    n_ffn = ffn_in[1].shape[1] // MXU_COLS + D_MODEL // MXU_COLS
    n_head = mix_in[2].shape[1] // RET_WIDTH + 2 * (TS // CHUNK) + 1
    n_tail = D_MODEL // MXU_COLS
    _emit("MF" * n_head + "F" * (n_ffn - n_head) + "M" * n_tail, {"F": ffn, "M": mixer})


def _odd_kernel(*refs, n_mix_in, n_cast, tiles_per_seq, final):
    x_ref, mix_in, ffn_in, o_ref, (h1_ref, hnf_ref, act_ref), mix_scratch, cast = _split_refs(
        refs, n_mix_in, n_cast)
    ng_ref, w1_ref, b1_ref, wdw_ref, bdw_ref, lng_ref, lnb_ref, w2_ref, b2_ref = mix_in
    hn_ref, z_ref, e_ref, y_ref, carry_ref, hn2_ref = mix_scratch
    n = pl.program_id(0)

    @pl.when(n == 0)
    def _():
        h1_ref[...] = jnp.zeros((TS, D_MODEL), f32)

    _odd_init(lax.rem(n, tiles_per_seq) == 0, carry_ref)
    ffn = _ffn_stream(h1_ref, *ffn_in, hnf_ref, act_ref, o_ref, final)
    glu = _odd_glu_stream(x_ref, ng_ref, w1_ref, b1_ref, hn_ref, z_ref, carry_ref)
    conv = _odd_dwconv_stream(wdw_ref, bdw_ref, z_ref, e_ref, y_ref)
    out = _odd_out_stream(x_ref.at[0], h1_ref, lng_ref, lnb_ref, w2_ref, b2_ref, hn2_ref, y_ref)
    next(ffn)
    _cast_blocks(cast)
    n_gate_up = ffn_in[1].shape[1] // MXU_COLS
    n_blk = D_MODEL // MXU_COLS
    order = "GFF" + "GFFF" * (n_blk - 1) + "G" + "F" * n_blk + "C" * n_blk + "O" * (n_blk + 1)
    assert (order.count("F"), order.count("G")) == (n_gate_up + n_blk, n_blk + 1)
    _emit(order, {"F": ffn, "G": glu, "C": conv, "O": out})


def _whole(arr):
    zeros = (0,) * arr.ndim
    return arr, pl.BlockSpec(arr.shape, lambda n: zeros, pipeline_mode=pl.Buffered(1))


def _stacked(arr, idx):
    if arr.ndim == 2:
        arr = arr[:, None, :]
    zeros = (0,) * (arr.ndim - 1)
    return arr, pl.BlockSpec((None,) + arr.shape[1:], lambda n: (idx,) + zeros, pipeline_mode=pl.Buffered(1))


def _cast_specs(arr, idx, n_steps):
    _, R, C = arr.shape
    rows = next(r for r in range(2 * SUBLANES, R + 1, 2 * SUBLANES) if R % r == 0 and R // r <= n_steps)
    last = R // rows - 1
    return (pl.BlockSpec((None, rows, C), lambda n: (idx, jnp.minimum(n, last), 0)),
            pl.BlockSpec((rows, C), lambda n: (jnp.minimum(n, last), 0)),
            jax.ShapeDtypeStruct((R, C), bf16))


def _layer(kind, h, mix, mix_scratch, ffn, cast, final):
    B, S, D = h.shape
    nt = B * S // TS
    d_ff = ffn[1][0].shape[-1]
    last = nt - 1
    tile_in = lambda n: (jnp.minimum(n, last), 0, 0)
    tile_out = lambda n: (jnp.maximum(n - 1, 0), 0, 0)
    static = dict(n_mix_in=len(mix), n_cast=len(cast), tiles_per_seq=S // TS, final=final)
    if kind == "even":
        body = functools.partial(_even_kernel, chunk_decay=_chunk_decay(), **static)
    else:
        body = functools.partial(_odd_kernel, **static)
    mix_specs = [pl.BlockSpec((1, TS, a.shape[-1]), tile_in) if s == "tile" else s for a, s in mix]
    cast_specs = [_cast_specs(a, idx, nt + 1) for a, idx in cast]
    outs = pl.pallas_call(
        body,
        grid=(nt + 1,),
        in_specs=[pl.BlockSpec((1, TS, D), tile_in)] + mix_specs + [s for _, s in ffn]
        + [s[0] for s in cast_specs],
        out_specs=[pl.BlockSpec((1, TS, D), tile_out)] + [s[1] for s in cast_specs],
        out_shape=[jax.ShapeDtypeStruct((nt, TS, D), h.dtype)] + [s[2] for s in cast_specs],
        scratch_shapes=[pltpu.VMEM((TS, D), f32), pltpu.VMEM((TS, D), bf16), pltpu.VMEM((TS, d_ff), bf16)]
        + mix_scratch,
        compiler_params=pltpu.CompilerParams(dimension_semantics=("arbitrary",),
                                             vmem_limit_bytes=VMEM_LIMIT_BYTES),
        name=kind + ("_layer_final" if final else "_layer"),
    )(h.reshape(nt, TS, D), *[a for a, _ in mix], *[a for a, _ in ffn], *[a for a, _ in cast])
    return outs[0].reshape(B, S, D), outs[1:]


def _even_layer(h, pos, mix_params, ffn, cast, final):
    B, S, D = h.shape
    in_width = mix_params[1][0].shape[-1]
    half = HEAD_DIM // 2
    inv_freq = ROPE_BASE ** (-jnp.arange(half, dtype=f32) / half)
    inv_freq = jnp.concatenate([inv_freq, inv_freq])[None, :]
    sign = jnp.concatenate([-jnp.ones((half,), f32), jnp.ones((half,), f32)])[None, :]
    norm_g, w_in, ret_norm_g, pool_w, pool_scale, w_out = mix_params
    mix = [(pos, "tile"), norm_g, w_in, _whole(inv_freq), _whole(sign)]
    mix += [_whole(jnp.asarray(c)) for c in _retention_constants()]
    mix += [ret_norm_g, pool_w, pool_scale, w_out]
    mix_scratch = [pltpu.VMEM((TS, D), bf16),
                   pltpu.VMEM((TS, in_width), f32),
                   pltpu.VMEM((N_HEADS, HEAD_DIM, HEAD_DIM), f32),
                   pltpu.VMEM((POOL_HALO + TS, POOL_WIDTH), f32),
                   pltpu.VMEM((TS, RET_WIDTH + POOL_WIDTH), bf16)]
    return _layer("even", h, mix, mix_scratch, ffn, cast, final)


def _odd_layer(h, mix_params, ffn, cast, final):
    B, S, D = h.shape
    mix_scratch = [pltpu.VMEM((TS, D), bf16),
                   pltpu.VMEM((N_SLABS, SUBLANES * Z_PITCH, LANES), f32),
                   pltpu.VMEM((N_SLABS, (SEG + CONV_WIDTH - 1) * SUBLANES, LANES), f32),
                   pltpu.VMEM((N_SLABS, SUBLANES * Y_PITCH, LANES), f32),
                   pltpu.VMEM((N_SLABS, HALO, LANES), f32),
                   pltpu.VMEM((TS, D), bf16)]
    return _layer("odd", h, list(mix_params), mix_scratch, ffn, cast, final)


def kernel(x, positions, mixer_norm_g, ffn_norm_g, final_norm_g, ret_w_in, ret_norm_g, pool_w, pool_scale, mix_w_out, conv_w_pw1, conv_b_pw1, conv_w_dw, conv_b_dw, conv_ln_g, conv_ln_b, conv_w_pw2, conv_b_pw2, ffn_w_gate, ffn_w_up, ffn_w_down):
    B, S, D = x.shape
    depth = mixer_norm_g.shape[0]
    nt = B * S // TS
    pos = jnp.broadcast_to(positions.reshape(nt, TS, 1), (nt, TS, HEAD_DIM))
    w_pool = pool_w.astype(bf16)

    def big_weights(layer):
        i = layer // 2
        mixer = [(ret_w_in, i), (mix_w_out, i)] if layer % 2 == 0 else [(conv_w_pw1, i), (conv_w_pw2, i)]
        return [(ffn_w_gate, layer), (ffn_w_up, layer), (ffn_w_down, layer)] + mixer

    weights = [a[idx].astype(bf16) for a, idx in big_weights(0)]
    h = x
    for layer in range(depth):
        i = layer // 2
        final = layer == depth - 1
        w_gate, w_up, w_down, w_mix_a, w_mix_b = [_whole(w) for w in weights]
        ffn = [_stacked(ffn_norm_g, layer), w_gate, w_up, w_down, _whole(final_norm_g[None, :])]
        cast = [] if final else big_weights(layer + 1)
        if layer % 2 == 0:
            mix_params = [_stacked(mixer_norm_g, layer), w_mix_a, _stacked(ret_norm_g, i),
                          _stacked(w_pool, i), _stacked(pool_scale, i), w_mix_b]
            h, weights = _even_layer(h, pos, mix_params, ffn, cast, final)
        else:
            mix_params = [_stacked(mixer_norm_g, layer), w_mix_a, _stacked(conv_b_pw1, i),
                          _stacked(conv_w_dw, i), _stacked(conv_b_dw, i), _stacked(conv_ln_g, i),
                          _stacked(conv_ln_b, i), w_mix_b, _stacked(conv_b_pw2, i)]
            h, weights = _odd_layer(h, mix_params, ffn, cast, final)
    return h
```
